```python
import math
import jax, jax.numpy as jnp
from jax import lax
import numpy as np

D_MODEL = 1024
BATCH = 8
SEQ = 4096
DEPTH = 1
DEC_BATCH = 32
DEC_SEQ = 1
PAST_LEN = 16384
PAGE_SIZE = 128

N_HEADS = 8
HEAD_DIM = 64
D_ATT = N_HEADS * HEAD_DIM
N_IDX_HEADS = 8
IDX_DIM = 64
TOPK_MAX = 256
D_RNN = D_MODEL - D_ATT
N_RNN_BLOCKS = 8
RNN_BLOCK = D_RNN // N_RNN_BLOCKS
CONV_W = 4
LRU_C = 8.0
N_BUCKETS = 32
MAX_DISTANCE = 128
D_FF = 4 * D_MODEL
DN_ALPHA = (2 * DEPTH) ** 0.25
DN_BETA = (8 * DEPTH) ** -0.25
LN_EPS = 1e-5
Q_BLOCK = 128
IN_SIZES = (D_ATT, D_ATT, D_ATT, N_IDX_HEADS * IDX_DIM, IDX_DIM, N_IDX_HEADS, D_RNN, D_RNN)
D_IN = sum(IN_SIZES)

kernel_name = "hymba_dsa_rglru_decoder_step"


def _split_in(proj):
    offs, acc = [], 0
    for s in IN_SIZES[:-1]:
        acc += s
        offs.append(acc)
    q, k, v, qi, ki, wi, xr, gr = jnp.split(proj, offs, axis=-1)
    lead = proj.shape[:-1]
    return (q.reshape(*lead, N_HEADS, HEAD_DIM), k.reshape(*lead, N_HEADS, HEAD_DIM),
            v.reshape(*lead, N_HEADS, HEAD_DIM), qi.reshape(*lead, N_IDX_HEADS, IDX_DIM),
            ki, wi, xr, gr)


def _layer_norm(x, g, b):
    xf = x.astype(jnp.float32)
    mu = jnp.mean(xf, axis=-1, keepdims=True)
    var = jnp.mean(jnp.square(xf - mu), axis=-1, keepdims=True)
    y = (xf - mu) * lax.rsqrt(var + LN_EPS) * g.astype(jnp.float32) + b.astype(jnp.float32)
    return y.astype(x.dtype)


def _t5_bucket(dist):
    dist = jnp.maximum(dist, 0)
    max_exact = N_BUCKETS // 2
    d = jnp.maximum(dist, 1).astype(jnp.float32)
    large = max_exact + (jnp.log(d / max_exact) / math.log(MAX_DISTANCE / max_exact)
                         * (N_BUCKETS - max_exact)).astype(jnp.int32)
    large = jnp.minimum(large, N_BUCKETS - 1)
    return jnp.where(dist < max_exact, dist, large)


def _take_rows(x, idx):
    return jax.vmap(lambda xb, ib: xb[ib])(x, idx)


def _indexer_scores(qi, wi, ki):
    dots = jnp.einsum('bqhd,bsd->bqhs', qi.astype(jnp.float32), ki.astype(jnp.float32))
    return jnp.einsum('bqh,bqhs->bqs', wi.astype(jnp.float32), jax.nn.relu(dots))


def _sparse_attend(q, k_sel, v_sel, bias, valid):
    logits = jnp.einsum('bqhd,bqkhd->bqhk', q.astype(jnp.float32), k_sel.astype(jnp.float32)) * (HEAD_DIM ** -0.5)
    logits = logits + jnp.moveaxis(bias.astype(jnp.float32), -1, 2)
    logits = jnp.where(valid[:, :, None, :], logits, -jnp.inf)
    p = jax.nn.softmax(logits, axis=-1)
    return jnp.einsum('bqhk,bqkhd->bqhd', p, v_sel.astype(jnp.float32)).astype(q.dtype)


def _prompt_attention(q, k, v, qi, ki, wi, rel_bias):
    B, S = q.shape[:2]
    topk = min(TOPK_MAX, S // 4)
    nb = S // Q_BLOCK

    def blocks(a):
        return jnp.moveaxis(a.reshape(B, nb, Q_BLOCK, *a.shape[2:]), 1, 0)

    key_pos = jnp.arange(S, dtype=jnp.int32)
    q_pos = key_pos.reshape(nb, Q_BLOCK)

    def one_block(args):
        q_b, qi_b, wi_b, pos_b = args
        scores = _indexer_scores(qi_b, wi_b, ki)
        causal = key_pos[None, :] <= pos_b[:, None]
        scores = jnp.where(causal[None], scores, -jnp.inf)
        _, idx = lax.top_k(scores, topk)
        dist = pos_b[None, :, None] - idx
        bias = rel_bias[_t5_bucket(dist)]
        return _sparse_attend(q_b, _take_rows(k, idx), _take_rows(v, idx), bias, dist >= 0)

    out = lax.map(one_block, (blocks(q), blocks(qi), blocks(wi), q_pos))
    return jnp.moveaxis(out, 0, 1).reshape(B, S, N_HEADS, HEAD_DIM)


def _sample_attention(q, k_new, v_new, qi, ki_new, wi, cache_k, cache_v, cache_ki, page_table, rel_bias):
    Bd, T = q.shape[:2]
    n_pages = PAST_LEN // PAGE_SIZE
    past = n_pages * PAGE_SIZE
    L = past + T
    topk = min(TOPK_MAX, L // 4)
    ki_past = cache_ki[page_table].reshape(Bd, past, IDX_DIM)
    ki_all = jnp.concatenate([ki_past, ki_new.astype(ki_past.dtype)], axis=1)
    q_pos = past + jnp.arange(T, dtype=jnp.int32)
    key_pos = jnp.arange(L, dtype=jnp.int32)
    scores = _indexer_scores(qi, wi, ki_all)
    scores = jnp.where(key_pos[None, None, :] <= q_pos[None, :, None], scores, -jnp.inf)
    _, idx = lax.top_k(scores, topk)
    from_past = (idx < past)[..., None, None]
    pidx = jnp.minimum(idx, past - 1)
    phys = jnp.take_along_axis(page_table, (pidx // PAGE_SIZE).reshape(Bd, -1), axis=1).reshape(idx.shape)
    off = pidx % PAGE_SIZE
    nidx = jnp.clip(idx - past, 0, T - 1)
    k_sel = jnp.where(from_past, cache_k[phys, off], _take_rows(k_new.astype(cache_k.dtype), nidx))
    v_sel = jnp.where(from_past, cache_v[phys, off], _take_rows(v_new.astype(cache_v.dtype), nidx))
    dist = q_pos[None, :, None] - idx
    bias = rel_bias[_t5_bucket(dist)]
    return _sparse_attend(q, k_sel, v_sel, bias, dist >= 0)


def _rglru_branch(xr, conv_prev, h_prev, conv_w, conv_b, w_a, b_a, w_x, b_x, lru_lambda):
    B, T = xr.shape[:2]
    x_ext = jnp.concatenate([conv_prev.astype(xr.dtype), xr], axis=1)
    xc = conv_b + x_ext[:, 0:T] * conv_w[0]
    for j in range(1, CONV_W):
        xc = xc + x_ext[:, j:j + T] * conv_w[j]
    new_conv = x_ext[:, T:]
    xb = xc.reshape(B, T, N_RNN_BLOCKS, RNN_BLOCK)
    r = jax.nn.sigmoid((jnp.einsum('btnc,ncd->btnd', xb, w_a) + b_a).astype(jnp.float32)).reshape(B, T, D_RNN)
    i = jax.nn.sigmoid((jnp.einsum('btnc,ncd->btnd', xb, w_x) + b_x).astype(jnp.float32)).reshape(B, T, D_RNN)
    log_a = -LRU_C * r * jax.nn.softplus(-lru_lambda.astype(jnp.float32))
    a = jnp.exp(log_a)
    u = jnp.sqrt(-jnp.expm1(2.0 * log_a)) * i * xc.astype(jnp.float32)
    u = u.at[:, 0].add(a[:, 0] * h_prev.astype(jnp.float32))

    def combine(left, right):
        a1, b1 = left
        a2, b2 = right
        return a1 * a2, a2 * b1 + b2

    _, h = lax.associative_scan(combine, (a, u), axis=1)
    return h, h[:, -1], new_conv


def _layer(x, attention_fn, conv_prev, h_prev, w_in, conv_w, conv_b, w_a, b_a, w_x, b_x, lru_lambda,
           w_out, ln1_g, ln1_b, w_up, b_up, w_down, b_down, ln2_g, ln2_b):
    B, T, _ = x.shape
    q, k, v, qi, ki, wi, xr, gr = _split_in(x @ w_in)
    attn = attention_fn(q, k, v, qi, ki, wi).reshape(B, T, D_ATT)
    h_seq, h_last, new_conv = _rglru_branch(xr, conv_prev, h_prev, conv_w, conv_b, w_a, b_a, w_x, b_x, lru_lambda)
    rnn = h_seq.astype(x.dtype) * jax.nn.gelu(gr)
    mix = jnp.concatenate([attn.astype(x.dtype), rnn], axis=-1) @ w_out
    x1 = _layer_norm(DN_ALPHA * x + mix, ln1_g, ln1_b)
    hid = jnp.square(jax.nn.relu(x1 @ w_up + b_up))
    x2 = _layer_norm(DN_ALPHA * x1 + hid @ w_down + b_down, ln2_g, ln2_b)
    return x2, (k, v, ki, h_last.astype(x.dtype), new_conv)


def setup_inputs(seed: int = 0) -> dict:
    key = jax.random.key(seed)
    ks = jax.random.split(key, 32)
    f32 = jnp.float32
    n_pages = PAST_LEN // PAGE_SIZE
    n_pool = (5 * DEC_BATCH * n_pages) // 4

    def nrm(k, shape, s=1.0):
        return s * jax.random.normal(k, shape, f32)

    x_prompt = nrm(ks[0], (BATCH, SEQ, D_MODEL))
    x_sample = nrm(ks[1], (DEC_BATCH, DEC_SEQ, D_MODEL))
    cache_k = nrm(ks[2], (DEPTH, n_pool, PAGE_SIZE, N_HEADS, HEAD_DIM))
    cache_v = nrm(ks[3], (DEPTH, n_pool, PAGE_SIZE, N_HEADS, HEAD_DIM))
    cache_k_idx = nrm(ks[4], (DEPTH, n_pool, PAGE_SIZE, IDX_DIM))
    state_h = nrm(ks[5], (DEPTH, DEC_BATCH, D_RNN), 0.5)
    state_conv = nrm(ks[6], (DEPTH, DEC_BATCH, CONV_W - 1, D_RNN))
    page_table = jax.random.permutation(ks[7], n_pool)[: DEC_BATCH * n_pages].reshape(DEC_BATCH, n_pages).astype(jnp.int32)
    rel_bias = nrm(ks[8], (N_BUCKETS, N_HEADS), 0.5)
    w_in = nrm(ks[9], (DEPTH, D_MODEL, D_IN), D_MODEL ** -0.5)
    conv_w = nrm(ks[10], (DEPTH, CONV_W, D_RNN), CONV_W ** -0.5)
    conv_b = nrm(ks[11], (DEPTH, D_RNN), 0.01)
    w_a = nrm(ks[12], (DEPTH, N_RNN_BLOCKS, RNN_BLOCK, RNN_BLOCK), RNN_BLOCK ** -0.5)
    b_a = nrm(ks[13], (DEPTH, N_RNN_BLOCKS, RNN_BLOCK), 0.01)
    w_x = nrm(ks[14], (DEPTH, N_RNN_BLOCKS, RNN_BLOCK, RNN_BLOCK), RNN_BLOCK ** -0.5)
    b_x = nrm(ks[15], (DEPTH, N_RNN_BLOCKS, RNN_BLOCK), 0.01)
    u = jax.random.uniform(ks[16], (DEPTH, D_RNN), f32, 0.9, 0.999)
    s = u ** (1.0 / LRU_C)
    lru_lambda = jnp.log(s) - jnp.log1p(-s)
    w_out = nrm(ks[17], (DEPTH, D_MODEL, D_MODEL), DN_BETA * D_MODEL ** -0.5)
    ln1_g = 1.0 + nrm(ks[18], (DEPTH, D_MODEL), 0.01)
    ln1_b = nrm(ks[19], (DEPTH, D_MODEL), 0.01)
    w_up = nrm(ks[20], (DEPTH, D_MODEL, D_FF), D_MODEL ** -0.5)
    b_up = nrm(ks[21], (DEPTH, D_FF), 0.01)
    w_down = nrm(ks[22], (DEPTH, D_FF, D_MODEL), DN_BETA * D_FF ** -0.5)
    b_down = nrm(ks[23], (DEPTH, D_MODEL), 0.01)
    ln2_g = 1.0 + nrm(ks[24], (DEPTH, D_MODEL), 0.01)
    ln2_b = nrm(ks[25], (DEPTH, D_MODEL), 0.01)
    return {"x_prompt": x_prompt, "x_sample": x_sample, "cache_k": cache_k, "cache_v": cache_v,
            "cache_k_idx": cache_k_idx, "state_h": state_h, "state_conv": state_conv,
            "page_table": page_table, "rel_bias": rel_bias, "w_in": w_in, "conv_w": conv_w,
            "conv_b": conv_b, "w_a": w_a, "b_a": b_a, "w_x": w_x, "b_x": b_x,
            "lru_lambda": lru_lambda, "w_out": w_out, "ln1_g": ln1_g, "ln1_b": ln1_b,
            "w_up": w_up, "b_up": b_up, "w_down": w_down, "b_down": b_down,
            "ln2_g": ln2_g, "ln2_b": ln2_b}


def reference(x_prompt, x_sample, cache_k, cache_v, cache_k_idx, state_h, state_conv, page_table,
              rel_bias, w_in, conv_w, conv_b, w_a, b_a, w_x, b_x, lru_lambda, w_out,
              ln1_g, ln1_b, w_up, b_up, w_down, b_down, ln2_g, ln2_b):
    y_p, y_s = x_prompt, x_sample
    kp, vp, kip, hp, cp = [], [], [], [], []
    ks_, vs_, kis, hs, cs = [], [], [], [], []
    zero_conv = jnp.zeros((x_prompt.shape[0], CONV_W - 1, D_RNN), x_prompt.dtype)
    zero_h = jnp.zeros((x_prompt.shape[0], D_RNN), x_prompt.dtype)
    for layer in range(DEPTH):
        lw = (w_in[layer], conv_w[layer], conv_b[layer], w_a[layer], b_a[layer], w_x[layer], b_x[layer],
              lru_lambda[layer], w_out[layer], ln1_g[layer], ln1_b[layer], w_up[layer], b_up[layer],
              w_down[layer], b_down[layer], ln2_g[layer], ln2_b[layer])
        prompt_attn = lambda q, k, v, qi, ki, wi: _prompt_attention(q, k, v, qi, ki, wi, rel_bias)
        ck, cv, cki = cache_k[layer], cache_v[layer], cache_k_idx[layer]
        sample_attn = lambda q, k, v, qi, ki, wi, ck=ck, cv=cv, cki=cki: _sample_attention(
            q, k, v, qi, ki, wi, ck, cv, cki, page_table, rel_bias)
        y_p, (k1, v1, ki1, h1, c1) = _layer(y_p, prompt_attn, zero_conv, zero_h, *lw)
        y_s, (k2, v2, ki2, h2, c2) = _layer(y_s, sample_attn, state_conv[layer], state_h[layer], *lw)
        kp.append(k1); vp.append(v1); kip.append(ki1); hp.append(h1); cp.append(c1)
        ks_.append(k2); vs_.append(v2); kis.append(ki2); hs.append(h2); cs.append(c2)
    k_prompt, v_prompt, k_idx_prompt = jnp.stack(kp), jnp.stack(vp), jnp.stack(kip)
    h_prompt, conv_prompt = jnp.stack(hp), jnp.stack(cp)
    k_sample, v_sample, k_idx_sample = jnp.stack(ks_), jnp.stack(vs_), jnp.stack(kis)
    h_sample, conv_sample = jnp.stack(hs), jnp.stack(cs)
    return (y_p, y_s, k_prompt, v_prompt, k_idx_prompt, h_prompt, conv_prompt,
            k_sample, v_sample, k_idx_sample, h_sample, conv_sample)
```

```python
import functools
import math

import jax
import jax.numpy as jnp
from jax import lax
from jax.experimental import pallas as pl
from jax.experimental.pallas import tpu as pltpu

F32, BF16, I32 = jnp.float32, jnp.bfloat16, jnp.int32

N_HEADS = 8
HEAD_DIM = 64
D_ATT = N_HEADS * HEAD_DIM
N_IDX_HEADS = 8
IDX_DIM = 64
TOPK_MAX = 256
N_RNN_BLOCKS = 8
CONV_W = 4
LRU_C = 8.0
N_BUCKETS = 32
MAX_DISTANCE = 128
LN_EPS = 1e-5
PAGE_SIZE = 128

LANES = 128
SUBLANES = 8
VMEM_LIMIT = 56 * 1024 * 1024

INT_MIN = -(2 ** 31)
NEG_INF = float("-inf")

TQ = 256
KC = 256


def _nt_dot(a, b, precision=None):
    return lax.dot_general(a, b, (((1,), (1,)), ((), ())), precision=precision,
                           preferred_element_type=F32)


def _dot(a, b, precision=None):
    return jnp.dot(a, b, precision=precision, preferred_element_type=F32)


def _t5_bucket(dist):
    dist = jnp.maximum(dist, 0)
    max_exact = N_BUCKETS // 2
    d = jnp.maximum(dist, 1).astype(F32)
    large = max_exact + (jnp.log(d / max_exact) / math.log(MAX_DISTANCE / max_exact)
                         * (N_BUCKETS - max_exact)).astype(I32)
    large = jnp.minimum(large, N_BUCKETS - 1)
    return jnp.where(dist < max_exact, dist, large)


def _sortable_key(s):
    bits = lax.bitcast_convert_type(s, I32)
    return jnp.where(bits >= 0, bits, jnp.int32(INT_MIN) - bits)


def _proj_body(x_ref, *refs, plan, precision):
    n_w = len(plan)
    w_refs, o_refs = refs[:n_w], refs[n_w:]
    x = x_ref[...].astype(w_refs[0].dtype)
    oi = 0
    for (transposed, outs), w_ref in zip(plan, w_refs):
        if transposed:
            y = _nt_dot(w_ref[...], x, precision)
        else:
            y = _dot(x, w_ref[...], precision)
        for lo, hi, scale, dtype in outs:
            part = y[lo:hi, :] if transposed else y[:, lo:hi]
            if scale != 1.0:
                part = part * scale
            o_refs[oi][...] = part.astype(dtype)
            oi += 1


def _project(x2d, weights, plan, tm, precision=None):
    n, d = x2d.shape
    assert n % tm == 0
    in_specs = [pl.BlockSpec((tm, d), lambda i: (i, 0))]
    for w in weights:
        in_specs.append(pl.BlockSpec(w.shape, lambda i: (0, 0)))
    out_shapes, out_specs = [], []
    for (transposed, outs) in plan:
        for lo, hi, _, dtype in outs:
            if transposed:
                out_shapes.append(jax.ShapeDtypeStruct((hi - lo, n), dtype))
                out_specs.append(pl.BlockSpec((hi - lo, tm), lambda i: (0, i)))
            else:
                out_shapes.append(jax.ShapeDtypeStruct((n, hi - lo), dtype))
                out_specs.append(pl.BlockSpec((tm, hi - lo), lambda i: (i, 0)))
    return pl.pallas_call(
        functools.partial(_proj_body, plan=plan, precision=precision),
        grid=(n // tm,),
        in_specs=in_specs,
        out_specs=out_specs,
        out_shape=out_shapes,
        compiler_params=pltpu.CompilerParams(dimension_semantics=("arbitrary",),
                                             vmem_limit_bytes=VMEM_LIMIT),
        name="in_proj",
    )(x2d, *weights)


def _count_rows(keys_ref, nch, pred):
    tq = keys_ref.shape[1]

    def body(j, acc):
        off = pl.multiple_of(j * KC, KC)
        m = pred(keys_ref[pl.ds(off, KC), :], off)
        part = jnp.where(m, 1, 0).astype(I32)
        return acc + jnp.sum(part.reshape(KC // SUBLANES, SUBLANES, tq), axis=0)

    acc = lax.fori_loop(0, nch, body, jnp.zeros((SUBLANES, tq), I32))
    return jnp.sum(acc, axis=0, keepdims=True)


def _select_threshold(keys_ref, nch, topk, idx_bits):
    tq = keys_ref.shape[1]
    c0 = _count_rows(keys_ref, nch, lambda k, off: k >= 0)
    nonneg = c0 >= topk
    prefix = jnp.where(nonneg, 0, INT_MIN).astype(I32)
    cge = jnp.where(nonneg, c0, nch * KC).astype(I32)

    def bit_body(it, carry):
        prefix, cge = carry
        cand = prefix | lax.shift_left(jnp.int32(1), 30 - it)
        c = _count_rows(keys_ref, nch, lambda k, off: k >= cand)
        ok = c >= topk
        return jnp.where(ok, cand, prefix), jnp.where(ok, c, cge)

    thr, cge = lax.fori_loop(0, 31, bit_body, (prefix, cge))
    real = thr > INT_MIN
    need = jnp.logical_and(cge > topk, real)
    j_default = jnp.where(real, jnp.int32(2 ** 30), jnp.int32(-1))

    def tie_fn():
        cgt = _count_rows(keys_ref, nch, lambda k, off: k > thr)
        want = topk - cgt

        def body(it, p):
            cand = p | lax.shift_left(jnp.int32(1), idx_bits - 1 - it)

            def pred(k, off):
                kidx = off + lax.broadcasted_iota(I32, (KC, 1), 0)
                return jnp.logical_and(k == thr, kidx < cand)

            c = _count_rows(keys_ref, nch, pred)
            return jnp.where(c < want, cand, p)

        p = lax.fori_loop(0, idx_bits, body, jnp.zeros((1, tq), I32))
        return jnp.where(need, p, j_default)

    any_need = jnp.max(jnp.where(need, 1, 0)) > 0
    jcut = lax.cond(any_need, tie_fn, lambda: j_default)
    return thr, jcut


def _prompt_attn_body(q_ref, qi_ref, wt_ref, k_ref, ki_ref, vt_ref, bias_ref, o_ref,
                      keys_ref, madd_ref, qm_ref, qim_ref, *, topk, idx_bits):
    i = pl.program_id(1)
    nch = i + 1
    tq = q_ref.shape[0]

    lane_hi = lax.broadcasted_iota(I32, (1, LANES), 1) >= HEAD_DIM
    for h in range(N_HEADS):
        hp, sub = divmod(h, 2)
        keep = lane_hi if sub == 1 else jnp.logical_not(lane_hi)
        sl = slice(hp * LANES, (hp + 1) * LANES)
        qm_ref[h] = jnp.where(keep, q_ref[:, sl], jnp.zeros((), BF16))
        qim_ref[h] = jnp.where(keep, qi_ref[:, sl], jnp.zeros((), BF16))

    wt = wt_ref[...]
    qidx = i * tq + lax.broadcasted_iota(I32, (1, tq), 1)

    def score_body(j, _):
        off = pl.multiple_of(j * KC, KC)
        kc = ki_ref[pl.ds(off, KC), :]
        s = jnp.zeros((KC, tq), F32)
        for h in range(N_IDX_HEADS):
            d = _nt_dot(kc, qim_ref[h])
            s = s + wt[h:h + 1, :] * jnp.maximum(d, 0.0)
        kidx = off + lax.broadcasted_iota(I32, (KC, 1), 0)
        keys_ref[pl.ds(off, KC), :] = jnp.where(kidx <= qidx, _sortable_key(s), INT_MIN)
        return 0

    lax.fori_loop(0, nch, score_body, 0)

    thr, jcut = _select_threshold(keys_ref, nch, topk, idx_bits)

    def madd_body(j, _):
        off = pl.multiple_of(j * KC, KC)
        k = keys_ref[pl.ds(off, KC), :]
        kidx = off + lax.broadcasted_iota(I32, (KC, 1), 0)
        sel = jnp.logical_or(k > thr, jnp.logical_and(k == thr, kidx <= jcut))
        madd_ref[pl.ds(off, KC), :] = jnp.where(sel, 0.0, NEG_INF).astype(F32)
        return 0

    lax.fori_loop(0, nch, madd_body, 0)

    def pair_body(hp, _):
        loff = pl.multiple_of(hp * LANES, LANES)

        def chunk_step(j, carry, bias_slot):
            off = pl.multiple_of(j * KC, KC)
            ksl = k_ref[pl.ds(off, KC), pl.ds(loff, LANES)]
            md = madd_ref[pl.ds(off, KC), :]
            out = []
            for sub in range(2):
                h = 2 * hp + sub
                m, l, acc = carry[sub]
                lg = _nt_dot(ksl, qm_ref[h]) + md
                if bias_slot is not None:
                    lg = lg + bias_ref[bias_slot, h]
                m_new = jnp.maximum(m, jnp.max(lg, axis=0, keepdims=True))
                m_use = jnp.where(m_new == NEG_INF, 0.0, m_new)
                p = jnp.exp(lg - m_use)
                alpha = jnp.exp(m - m_use)
                l_new = alpha * l + jnp.sum(p, axis=0, keepdims=True)
                vsl = vt_ref[pl.ds(pl.multiple_of(h * HEAD_DIM, HEAD_DIM), HEAD_DIM), pl.ds(off, KC)]
                acc_new = alpha * acc + _dot(vsl, p.astype(BF16))
                out.append((m_new, l_new, acc_new))
            return tuple(out)

        one = (jnp.full((1, tq), NEG_INF, F32), jnp.zeros((1, tq), F32), jnp.zeros((HEAD_DIM, tq), F32))
        carry = lax.fori_loop(0, jnp.maximum(i - 1, 0), lambda j, c: chunk_step(j, c, None), (one, one))
        carry = lax.cond(i >= 1, lambda c: chunk_step(i - 1, c, 1), lambda c: c, carry)
        carry = chunk_step(i, carry, 0)
        for sub in range(2):
            m, l, acc = carry[sub]
            row = pl.multiple_of((2 * hp + sub) * HEAD_DIM, HEAD_DIM)
            o_ref[pl.ds(row, HEAD_DIM), :] = (acc / l).astype(o_ref.dtype)
        return 0

    lax.fori_loop(0, N_HEADS // 2, pair_body, 0)


def _prompt_attention(qb, qib, wit, kb, ki2b, vt, bias_t, batch, seq):
    assert seq % TQ == 0 and TQ == KC
    nq = seq // TQ
    topk = min(TOPK_MAX, seq // 4)
    idx_bits = max(1, (seq - 1).bit_length())
    n = batch * seq
    return pl.pallas_call(
        functools.partial(_prompt_attn_body, topk=topk, idx_bits=idx_bits),
        grid=(batch, nq),
        in_specs=[
            pl.BlockSpec((TQ, D_ATT), lambda b, i: (b * nq + i, 0)),
            pl.BlockSpec((TQ, N_IDX_HEADS * IDX_DIM), lambda b, i: (b * nq + i, 0)),
            pl.BlockSpec((N_IDX_HEADS, TQ), lambda b, i: (0, b * nq + i)),
            pl.BlockSpec((seq, D_ATT), lambda b, i: (b, 0)),
            pl.BlockSpec((seq, 2 * IDX_DIM), lambda b, i: (b, 0)),
            pl.BlockSpec((D_ATT, seq), lambda b, i: (0, b)),
            pl.BlockSpec(bias_t.shape, lambda b, i: (0, 0, 0, 0)),
        ],
        out_specs=pl.BlockSpec((D_ATT, TQ), lambda b, i: (0, b * nq + i)),
        out_shape=jax.ShapeDtypeStruct((D_ATT, n), BF16),
        scratch_shapes=[
            pltpu.VMEM((seq, TQ), I32),
            pltpu.VMEM((seq, TQ), F32),
            pltpu.VMEM((N_HEADS, TQ, LANES), BF16),
            pltpu.VMEM((N_IDX_HEADS, TQ, LANES), BF16),
        ],
        compiler_params=pltpu.CompilerParams(dimension_semantics=("arbitrary", "arbitrary"),
                                             vmem_limit_bytes=VMEM_LIMIT),
        name="prompt_attention",
    )(qb, qib, wit, kb, ki2b, vt, bias_t)


def _prompt_bias_tiles(rel_bias):
    tbl = rel_bias[_t5_bucket(jnp.arange(2 * TQ, dtype=I32))]
    far = rel_bias[_t5_bucket(jnp.full((1,), MAX_DISTANCE, I32))]
    tbl = jnp.where(jnp.arange(2 * TQ)[:, None] < MAX_DISTANCE, tbl - far, 0.0)
    c = jnp.arange(KC, dtype=I32)[:, None]
    r = jnp.arange(TQ, dtype=I32)[None, :]
    tiles = []
    for base in (0, TQ):
        dist = base + r - c
        t = jnp.where((dist >= 0)[..., None], tbl[jnp.clip(dist, 0, 2 * TQ - 1)], 0.0)
        tiles.append(jnp.moveaxis(t, -1, 0))
    return jnp.stack(tiles).astype(F32)


def _lru_gates(xc, wa_ref, ba_ref, wx_ref, bx_ref, lam_ref, precision):
    xm = xc.astype(wa_ref.dtype)
    r = jax.nn.sigmoid(_dot(xm, wa_ref[...], precision) + ba_ref[...])
    g = jax.nn.sigmoid(_dot(xm, wx_ref[...], precision) + bx_ref[...])
    log_a = -LRU_C * r * jax.nn.softplus(-lam_ref[...])
    a = jnp.exp(log_a)
    u = jnp.sqrt(-jnp.tanh(log_a) * (a * a + 1.0)) * g * xc
    return a, u


def _rglru_prompt_body(xr_ref, gr_ref, cw_ref, cb_ref, wa_ref, ba_ref, wx_ref, bx_ref, lam_ref,
                       rnn_ref, hlast_ref, ext_ref, hc_ref):
    tt = xr_ref.shape[0]

    @pl.when(pl.program_id(1) == 0)
    def _():
        ext_ref[0:SUBLANES, :] = jnp.zeros((SUBLANES, ext_ref.shape[1]), F32)
        hc_ref[...] = jnp.zeros_like(hc_ref)

    x = xr_ref[...]
    ext_ref[SUBLANES:SUBLANES + tt, :] = x
    cw = cw_ref[...]
    xc = cb_ref[...] + ext_ref[SUBLANES - 3:SUBLANES - 3 + tt, :] * cw[0:1]
    for j in range(1, CONV_W):
        s = CONV_W - 1 - j
        xc = xc + ext_ref[SUBLANES - s:SUBLANES - s + tt, :] * cw[j:j + 1]
    ext_ref[0:SUBLANES, :] = x[tt - SUBLANES:tt]

    a, u = _lru_gates(xc, wa_ref, ba_ref, wx_ref, bx_ref, lam_ref, None)
    row = lax.broadcasted_iota(I32, (tt, 1), 0)
    d = 1
    while d < tt:
        keep = row >= d
        a_sh = jnp.where(keep, pltpu.roll(a, d, 0), 1.0)
        u_sh = jnp.where(keep, pltpu.roll(u, d, 0), 0.0)
        u = u + a * u_sh
        a = a * a_sh
        d *= 2
    h = u + a * hc_ref[...]
    h_last = h[tt - 1:tt]
    hc_ref[...] = h_last
    hlast_ref[0] = h_last
    rnn_ref[...] = (h * jax.nn.gelu(gr_ref[...])).astype(rnn_ref.dtype)


def _rglru_prompt(xr, gr, cw, cb, wa_bd, ba, wx_bd, bx, lam, batch, seq, tt):
    n, dr = xr.shape
    nt = seq // tt
    full = lambda shape: pl.BlockSpec(shape, lambda b, t: (0,) * len(shape))
    return pl.pallas_call(
        _rglru_prompt_body,
        grid=(batch, nt),
        in_specs=[
            pl.BlockSpec((tt, dr), lambda b, t: (b * nt + t, 0)),
            pl.BlockSpec((tt, dr), lambda b, t: (b * nt + t, 0)),
            full(cw.shape), full(cb.shape), full(wa_bd.shape), full(ba.shape),
            full(wx_bd.shape), full(bx.shape), full(lam.shape),
        ],
        out_specs=[
            pl.BlockSpec((tt, dr), lambda b, t: (b * nt + t, 0)),
            pl.BlockSpec((1, 1, dr), lambda b, t: (b, 0, 0)),
        ],
        out_shape=[jax.ShapeDtypeStruct((n, dr), BF16), jax.ShapeDtypeStruct((batch, 1, dr), F32)],
        scratch_shapes=[pltpu.VMEM((tt + SUBLANES, dr), F32), pltpu.VMEM((1, dr), F32)],
        compiler_params=pltpu.CompilerParams(dimension_semantics=("arbitrary", "arbitrary"),
                                             vmem_limit_bytes=VMEM_LIMIT),
        name="rglru_prompt",
    )(xr, gr, cw, cb, wa_bd, ba, wx_bd, bx, lam)


def _rglru_sample_body(xr_ref, gr_ref, cp_ref, hp_ref, cw_ref, cb_ref, wa_ref, ba_ref, wx_ref, bx_ref,
                       lam_ref, rnn_ref, h_ref):
    cw = cw_ref[...]
    xc = cb_ref[...] + cp_ref[0] * cw[0:1]
    for j in range(1, CONV_W - 1):
        xc = xc + cp_ref[j] * cw[j:j + 1]
    xc = xc + xr_ref[...] * cw[CONV_W - 1:CONV_W]
    a, u = _lru_gates(xc, wa_ref, ba_ref, wx_ref, bx_ref, lam_ref, lax.Precision.HIGHEST)
    h = u + a * hp_ref[...]
    h_ref[...] = h
    rnn_ref[...] = h * jax.nn.gelu(gr_ref[...])


def _rglru_sample(xr, gr, conv_prev_t, h_prev, cw, cb, wa_bd, ba, wx_bd, bx, lam):
    n, dr = xr.shape
    return pl.pallas_call(
        _rglru_sample_body,
        out_shape=[jax.ShapeDtypeStruct((n, dr), F32), jax.ShapeDtypeStruct((n, dr), F32)],
        compiler_params=pltpu.CompilerParams(vmem_limit_bytes=VMEM_LIMIT),
        name="rglru_sample",
    )(xr, gr, conv_prev_t, h_prev, cw, cb, wa_bd, ba, wx_bd, bx, lam)


def _layer_norm(x, g, b):
    mu = jnp.mean(x, axis=-1, keepdims=True)
    xc = x - mu
    var = jnp.mean(jnp.square(xc), axis=-1, keepdims=True)
    return xc * lax.rsqrt(var + LN_EPS) * g + b


def _mlp_body(attn_ref, rnn_ref, x_ref, woa_ref, wor_ref, g1_ref, b1_ref, wup_ref, bup_ref,
              wdn_ref, bdn_ref, g2_ref, b2_ref, y_ref, *, alpha, attn_channel_major, ff_chunk):
    cdt = woa_ref.dtype
    if attn_channel_major:
        mix = lax.dot_general(attn_ref[...].astype(cdt), woa_ref[...], (((0,), (0,)), ((), ())),
                              preferred_element_type=F32)
    else:
        mix = _dot(attn_ref[...].astype(cdt), woa_ref[...])
    mix = mix + _dot(rnn_ref[...].astype(cdt), wor_ref[...])
    x1 = _layer_norm(alpha * x_ref[...] + mix, g1_ref[...], b1_ref[...])
    x1c = x1.astype(cdt)
    d_ff = wup_ref.shape[1]
    y = None
    for c in range(0, d_ff, ff_chunk):
        hid = _dot(x1c, wup_ref[:, c:c + ff_chunk]) + bup_ref[:, c:c + ff_chunk]
        hid = jnp.square(jnp.maximum(hid, 0.0)).astype(cdt)
        part = _dot(hid, wdn_ref[c:c + ff_chunk, :])
        y = part if y is None else y + part
    y_ref[...] = _layer_norm(alpha * x1 + y + bdn_ref[...], g2_ref[...], b2_ref[...])


def _out_mlp(attn, rnn, x2d, woa, wor, g1, b1, wup, bup, wdn, bdn, g2, b2, *, alpha, tm,
             attn_channel_major):
    n, dm = x2d.shape
    da = woa.shape[0]
    full = lambda a: pl.BlockSpec(a.shape, lambda i: (0,) * a.ndim)
    attn_spec = (pl.BlockSpec((da, tm), lambda i: (0, i)) if attn_channel_major
                 else pl.BlockSpec((tm, da), lambda i: (i, 0)))
    consts = (woa, wor, g1, b1, wup, bup, wdn, bdn, g2, b2)
    return pl.pallas_call(
        functools.partial(_mlp_body, alpha=alpha, attn_channel_major=attn_channel_major, ff_chunk=1024),
        grid=(n // tm,),
        in_specs=[attn_spec, pl.BlockSpec((tm, rnn.shape[1]), lambda i: (i, 0)),
                  pl.BlockSpec((tm, dm), lambda i: (i, 0))] + [full(a) for a in consts],
        out_specs=pl.BlockSpec((tm, dm), lambda i: (i, 0)),
        out_shape=jax.ShapeDtypeStruct((n, dm), F32),
        compiler_params=pltpu.CompilerParams(dimension_semantics=("arbitrary",),
                                             vmem_limit_bytes=VMEM_LIMIT),
        name="out_mlp",
    )(attn, rnn, x2d, *consts)


def _sample_select_body(pt_ref, qi_ref, w_ref, kin_ref, cki_ref, idx_ref, kibuf, keys_ref, sem,
                        *, layer, topk, idx_bits):
    b = pl.program_id(0)
    n_pages = kibuf.shape[0]
    n_past = n_pages * PAGE_SIZE

    def page_copy(p, phys):
        return pltpu.make_async_copy(cki_ref.at[layer, phys], kibuf.at[p], sem)

    def issue(p, _):
        page_copy(p, pt_ref[b, p]).start()
        return 0

    lax.fori_loop(0, n_pages, issue, 0)

    def wait(p, _):
        page_copy(p, 0).wait()
        return 0

    lax.fori_loop(0, n_pages, wait, 0)

    qi16 = qi_ref[0].astype(BF16)
    w = w_ref[0]

    def score_body(g, _):
        pg = pl.multiple_of(g * 2, 2)
        kc = kibuf[pl.ds(pg, 2)].reshape(2 * PAGE_SIZE, IDX_DIM).astype(BF16)
        d = _nt_dot(qi16, kc)
        key = _sortable_key(jnp.sum(w * jnp.maximum(d, 0.0), axis=0, keepdims=True))
        keys_ref[pl.ds(pg, 1), :] = key[:, :PAGE_SIZE]
        keys_ref[pl.ds(pg + 1, 1), :] = key[:, PAGE_SIZE:]
        return 0

    lax.fori_loop(0, n_pages // 2, score_body, 0)

    dn = jnp.sum(qi16.astype(F32) * kin_ref[0].astype(BF16).astype(F32), axis=1, keepdims=True)
    key_new = _sortable_key(jnp.sum(w * jnp.maximum(dn, 0.0), axis=0, keepdims=True))

    keys = keys_ref[...]
    kidx = (lax.broadcasted_iota(I32, keys.shape, 0) * PAGE_SIZE
            + lax.broadcasted_iota(I32, keys.shape, 1))

    def total(x):
        return jnp.sum(jnp.sum(x, axis=0, keepdims=True), axis=1, keepdims=True)

    def count(pred_past, pred_new):
        return total(jnp.where(pred_past, 1, 0).astype(I32)) + jnp.where(pred_new, 1, 0).astype(I32)

    c0 = count(keys >= 0, key_new >= 0)
    nonneg = c0 >= topk
    prefix = jnp.where(nonneg, 0, INT_MIN).astype(I32)
    cge = jnp.where(nonneg, c0, n_past + 1).astype(I32)

    def bit_body(it, carry):
        prefix, cge = carry
        cand = prefix | lax.shift_left(jnp.int32(1), 30 - it)
        c = count(keys >= cand, key_new >= cand)
        ok = c >= topk
        return jnp.where(ok, cand, prefix), jnp.where(ok, c, cge)

    thr, cge = lax.fori_loop(0, 31, bit_body, (prefix, cge))
    real = thr > INT_MIN
    need = jnp.logical_and(cge > topk, real)
    j_default = jnp.where(real, jnp.int32(2 ** 30), jnp.int32(-1))

    def tie_fn():
        want = topk - count(keys > thr, key_new > thr)

        def body(it, p):
            cand = p | lax.shift_left(jnp.int32(1), idx_bits - 1 - it)
            c = count(jnp.logical_and(keys == thr, kidx < cand),
                      jnp.logical_and(key_new == thr, n_past < cand))
            return jnp.where(c < want, cand, p)

        p = lax.fori_loop(0, idx_bits, body, jnp.zeros((1, 1), I32))
        return jnp.where(need, p, j_default)

    jcut = lax.cond(jnp.max(jnp.where(need, 1, 0)) > 0, tie_fn, lambda: j_default)
    sel = jnp.logical_or(keys > thr, jnp.logical_and(keys == thr, kidx <= jcut))

    m16 = jnp.where(sel, 1.0, 0.0).astype(BF16)
    ii = lax.broadcasted_iota(I32, (PAGE_SIZE, PAGE_SIZE), 0)
    jj = lax.broadcasted_iota(I32, (PAGE_SIZE, PAGE_SIZE), 1)
    pre = _dot(m16, jnp.where(ii <= jj, 1.0, 0.0).astype(BF16))
    cnt_row = _nt_dot(jnp.ones((SUBLANES, PAGE_SIZE), BF16), m16)[0:1]
    pi = lax.broadcasted_iota(I32, (n_pages, n_pages), 0)
    pj = lax.broadcasted_iota(I32, (n_pages, n_pages), 1)
    off_row = _dot(jnp.broadcast_to(cnt_row, (SUBLANES, n_pages)).astype(BF16),
                   jnp.where(pi < pj, 1.0, 0.0).astype(BF16))[0:1]
    rank = (lax.broadcasted_iota(I32, (topk, 1), 0) + 1).astype(F32)
    page_sel = jnp.logical_and(off_row < rank, rank <= off_row + cnt_row)
    page_id = lax.broadcasted_iota(I32, (1, n_pages), 1).astype(F32)
    p_r = jnp.sum(jnp.where(page_sel, page_id, 0.0), axis=1, keepdims=True)
    q_r = rank - jnp.sum(jnp.where(page_sel, off_row, 0.0), axis=1, keepdims=True)
    ps16 = jnp.where(page_sel, 1.0, 0.0).astype(BF16)
    pre_r = _dot(ps16, pre.astype(BF16))
    m_r = _dot(ps16, m16)
    off_sel = jnp.logical_and(m_r > 0.5, pre_r == q_r)
    off_id = lax.broadcasted_iota(I32, (1, PAGE_SIZE), 1).astype(F32)
    o_r = jnp.sum(jnp.where(off_sel, off_id, 0.0), axis=1, keepdims=True)
    found = jnp.sum(jnp.where(page_sel, 1.0, 0.0), axis=1, keepdims=True) > 0.5
    idx_ref[0] = jnp.where(found, p_r * PAGE_SIZE + o_r, float(n_past)).astype(I32)


def _sample_select(page_table, qi_s, wi_col, ki_new, cache_k_idx, layer, topk):
    bd, n_pages = page_table.shape
    assert n_pages % SUBLANES == 0
    n_past = n_pages * PAGE_SIZE
    idx_bits = max(1, n_past.bit_length())
    grid_spec = pltpu.PrefetchScalarGridSpec(
        num_scalar_prefetch=1,
        grid=(bd,),
        in_specs=[
            pl.BlockSpec((1, N_IDX_HEADS, IDX_DIM), lambda b, pt: (b, 0, 0)),
            pl.BlockSpec((1, N_IDX_HEADS, 1), lambda b, pt: (b, 0, 0)),
            pl.BlockSpec((1, 1, IDX_DIM), lambda b, pt: (b, 0, 0)),
            pl.BlockSpec(memory_space=pl.ANY),
        ],
        out_specs=pl.BlockSpec((1, topk, 1), lambda b, pt: (b, 0, 0)),
        scratch_shapes=[
            pltpu.VMEM((n_pages, PAGE_SIZE, IDX_DIM), F32),
            pltpu.VMEM((n_pages, PAGE_SIZE), I32),
            pltpu.SemaphoreType.DMA(()),
        ],
    )
    return pl.pallas_call(
        functools.partial(_sample_select_body, layer=layer, topk=topk, idx_bits=idx_bits),
        grid_spec=grid_spec,
        out_shape=jax.ShapeDtypeStruct((bd, topk, 1), I32),
        compiler_params=pltpu.CompilerParams(dimension_semantics=("arbitrary",),
                                             vmem_limit_bytes=VMEM_LIMIT),
        name="sample_select",
    )(page_table, qi_s, wi_col, ki_new, cache_k_idx)


def _sample_attend_body(idx_ref, pt_ref, q_ref, kn_ref, vn_ref, btab_ref, ck_ref, cv_ref, o_ref,
                        kbuf, vbuf, bbuf, sem, *, layer, n_past):
    b = pl.program_id(0)
    topk = kbuf.shape[0]

    def row_copies(r, phys, off):
        return (pltpu.make_async_copy(ck_ref.at[layer, phys, off], kbuf.at[r], sem.at[0]),
                pltpu.make_async_copy(cv_ref.at[layer, phys, off], vbuf.at[r], sem.at[1]))

    def issue(r, _):
        key = idx_ref[b, r]
        pk = jnp.minimum(key, n_past - 1)
        phys = pt_ref[b, lax.shift_right_logical(pk, PAGE_SIZE.bit_length() - 1)]
        ck, cv = row_copies(r, phys, pk & (PAGE_SIZE - 1))
        ck.start()
        cv.start()
        bbuf[r] = btab_ref[jnp.minimum(n_past - key, MAX_DISTANCE)]
        return 0

    lax.fori_loop(0, topk, issue, 0)

    def wait(r, _):
        ck, cv = row_copies(r, 0, 0)
        ck.wait()
        cv.wait()
        return 0

    lax.fori_loop(0, topk, wait, 0)

    @pl.when(idx_ref[b, topk - 1] >= n_past)
    def _():
        kbuf[topk - 1] = kn_ref[0]
        vbuf[topk - 1] = vn_ref[0]

    q = q_ref[0]
    lg = jnp.sum(kbuf[...] * q[None], axis=-1, keepdims=True) + bbuf[:, :, 0:1]
    m = jnp.max(lg, axis=0, keepdims=True)
    p = jnp.exp(lg - m)
    l = jnp.sum(p, axis=0, keepdims=True)
    o_ref[0] = jnp.sum(p * vbuf[...], axis=0) / l[0]


def _sample_attend(idx, page_table, q_s, k_new, v_new, btab, cache_k, cache_v, layer):
    bd, topk = idx.shape
    n_past = page_table.shape[1] * PAGE_SIZE
    blk = lambda b, idx_r, pt: (b, 0, 0)
    grid_spec = pltpu.PrefetchScalarGridSpec(
        num_scalar_prefetch=2,
        grid=(bd,),
        in_specs=[
            pl.BlockSpec((1, N_HEADS, HEAD_DIM), blk),
            pl.BlockSpec((1, N_HEADS, HEAD_DIM), blk),
            pl.BlockSpec((1, N_HEADS, HEAD_DIM), blk),
            pl.BlockSpec(btab.shape, lambda b, idx_r, pt: (0, 0, 0)),
            pl.BlockSpec(memory_space=pl.ANY),
            pl.BlockSpec(memory_space=pl.ANY),
        ],
        out_specs=pl.BlockSpec((1, N_HEADS, HEAD_DIM), blk),
        scratch_shapes=[
            pltpu.VMEM((topk, N_HEADS, HEAD_DIM), F32),
            pltpu.VMEM((topk, N_HEADS, HEAD_DIM), F32),
            pltpu.VMEM((topk, N_HEADS, LANES), F32),
            pltpu.SemaphoreType.DMA((2,)),
        ],
    )
    return pl.pallas_call(
        functools.partial(_sample_attend_body, layer=layer, n_past=n_past),
        grid_spec=grid_spec,
        out_shape=jax.ShapeDtypeStruct((bd, N_HEADS, HEAD_DIM), F32),
        compiler_params=pltpu.CompilerParams(dimension_semantics=("arbitrary",),
                                             vmem_limit_bytes=VMEM_LIMIT),
        name="sample_attend",
    )(idx, page_table, q_s, k_new, v_new, btab, cache_k, cache_v)


def _block_diag(w):
    nb, c, d = w.shape
    eye = jnp.eye(nb, dtype=w.dtype)
    return (eye[:, None, :, None] * w[:, :, None, :]).reshape(nb * c, nb * d)


def kernel(x_prompt, x_sample, cache_k, cache_v, cache_k_idx, state_h, state_conv, page_table,
           rel_bias, w_in, conv_w, conv_b, w_a, b_a, w_x, b_x, lru_lambda, w_out,
           ln1_g, ln1_b, w_up, b_up, w_down, b_down, ln2_g, ln2_b):
    depth = w_in.shape[0]
    batch, seq, dm = x_prompt.shape
    bd, dec_seq, _ = x_sample.shape
    assert dec_seq == 1, "the decode pass handles one new token per sequence"
    d_rnn = dm - D_ATT
    alpha = (2 * depth) ** 0.25
    n_pages = page_table.shape[1]
    n_past = n_pages * PAGE_SIZE
    topk_s = min(TOPK_MAX, (n_past + dec_seq) // 4)
    row = lambda a: a.reshape(1, -1)

    sizes = (D_ATT, D_ATT, D_ATT, N_IDX_HEADS * IDX_DIM, IDX_DIM, N_IDX_HEADS, d_rnn, d_rnn)
    offs = [0]
    for s in sizes:
        offs.append(offs[-1] + s)

    bias_t = _prompt_bias_tiles(rel_bias)
    btab = jnp.broadcast_to(rel_bias[_t5_bucket(jnp.arange(MAX_DISTANCE + 1, dtype=I32))][:, :, None],
                            (MAX_DISTANCE + 1, N_HEADS, LANES)).astype(F32)

    yp = x_prompt.reshape(batch * seq, dm)
    ys = x_sample.reshape(bd, dm)
    outs = [[] for _ in range(10)]
    for layer in range(depth):
        w = w_in[layer]
        wq, wk, wv, wqi, wki, wwi, wxr, wgr = [w[:, offs[j]:offs[j + 1]] for j in range(8)]
        c16 = lambda a: a.astype(BF16)
        cw, cb, lam = conv_w[layer], row(conv_b[layer]), row(lru_lambda[layer])
        wa_bd, wx_bd = _block_diag(w_a[layer]), _block_diag(w_x[layer])
        ba, bx = row(b_a[layer]), row(b_x[layer])
        woa, wor = c16(w_out[layer][:D_ATT]), c16(w_out[layer][D_ATT:])
        mlp_consts = (woa, wor, row(ln1_g[layer]), row(ln1_b[layer]), c16(w_up[layer]), row(b_up[layer]),
                      c16(w_down[layer]), row(b_down[layer]), row(ln2_g[layer]), row(ln2_b[layer]))

        p_weights = [c16(wq), c16(wk), c16(wv), c16(wv.T), c16(wqi), c16(jnp.concatenate([wki, wki], 1)),
                     c16(wwi.T), c16(wxr), c16(wgr)]
        p_plan = (
            (False, ((0, D_ATT, HEAD_DIM ** -0.5, BF16),)),
            (False, ((0, D_ATT, 1.0, F32), (0, D_ATT, 1.0, BF16))),
            (False, ((0, D_ATT, 1.0, F32),)),
            (True, ((0, D_ATT, 1.0, BF16),)),
            (False, ((0, N_IDX_HEADS * IDX_DIM, 1.0, BF16),)),
            (False, ((0, IDX_DIM, 1.0, F32), (0, 2 * IDX_DIM, 1.0, BF16))),
            (True, ((0, N_IDX_HEADS, 1.0, F32),)),
            (False, ((0, d_rnn, 1.0, F32),)),
            (False, ((0, d_rnn, 1.0, F32),)),
        )
        qb, k_p, kb, v_p, vt, qib, ki_p, ki2b, wit, xr_p, gr_p = _project(yp, p_weights, p_plan, tm=512)
        attn_t = _prompt_attention(qb, qib, wit, kb, ki2b, vt, bias_t, batch, seq)
        rnn_p, h_p = _rglru_prompt(xr_p, gr_p, cw, cb, c16(wa_bd), ba, c16(wx_bd), bx, lam,
                                   batch, seq, tt=512)
        conv_p = xr_p.reshape(batch, seq, d_rnn)[:, seq - (CONV_W - 1):]
        yp = _out_mlp(attn_t, rnn_p, yp, *mlp_consts, alpha=alpha, tm=512, attn_channel_major=True)

        pad_wi = jnp.pad(wwi, ((0, 0), (0, LANES - N_IDX_HEADS)))
        s_weights = [wq, wk, wv, wqi, wki, pad_wi, wxr, wgr]
        s_plan = (
            (False, ((0, D_ATT, HEAD_DIM ** -0.5, F32),)),
            (False, ((0, D_ATT, 1.0, F32),)),
            (False, ((0, D_ATT, 1.0, F32),)),
            (False, ((0, N_IDX_HEADS * IDX_DIM, 1.0, F32),)),
            (False, ((0, IDX_DIM, 1.0, F32),)),
            (False, ((0, LANES, 1.0, F32),)),
            (False, ((0, d_rnn, 1.0, F32),)),
            (False, ((0, d_rnn, 1.0, F32),)),
        )
        q_s, k_s, v_s, qi_s, ki_s, wi_s, xr_s, gr_s = _project(ys, s_weights, s_plan, tm=bd,
                                                               precision=lax.Precision.HIGHEST)
        idx = _sample_select(page_table, qi_s.reshape(bd, N_IDX_HEADS, IDX_DIM),
                             wi_s[:, :N_IDX_HEADS].reshape(bd, N_IDX_HEADS, 1),
                             ki_s.reshape(bd, 1, IDX_DIM), cache_k_idx, layer, topk_s)
        heads = lambda a: a.reshape(bd, N_HEADS, HEAD_DIM)
        attn_s = _sample_attend(idx.reshape(bd, topk_s), page_table, heads(q_s), heads(k_s), heads(v_s),
                                btab, cache_k, cache_v, layer)
        conv_prev = state_conv[layer]
        rnn_s, h_s = _rglru_sample(xr_s, gr_s, jnp.swapaxes(conv_prev, 0, 1), state_h[layer], cw, cb,
                                   wa_bd, ba, wx_bd, bx, lam)
        conv_s = jnp.concatenate([conv_prev[:, 1:], xr_s[:, None, :]], axis=1)
        ys = _out_mlp(attn_s.reshape(bd, D_ATT), rnn_s, ys, *mlp_consts, alpha=alpha, tm=bd,
                      attn_channel_major=False)

        layer_outs = (
            k_p.reshape(batch, seq, N_HEADS, HEAD_DIM), v_p.reshape(batch, seq, N_HEADS, HEAD_DIM),
            ki_p.reshape(batch, seq, IDX_DIM), h_p.reshape(batch, d_rnn), conv_p,
            k_s.reshape(bd, dec_seq, N_HEADS, HEAD_DIM), v_s.reshape(bd, dec_seq, N_HEADS, HEAD_DIM),
            ki_s.reshape(bd, dec_seq, IDX_DIM), h_s, conv_s,
        )
        for lst, a in zip(outs, layer_outs):
            lst.append(a)

    stacked = [jnp.stack(lst) for lst in outs]
    return (yp.reshape(batch, seq, dm), ys.reshape(bd, dec_seq, dm), *stacked)
```

```python
import functools
import math

import jax
import jax.numpy as jnp
from jax import lax
from jax.experimental import pallas as pl
from jax.experimental.pallas import tpu as pltpu

F32, BF16, I32 = jnp.float32, jnp.bfloat16, jnp.int32

N_HEADS = 8
HEAD_DIM = 64
D_ATT = N_HEADS * HEAD_DIM
N_IDX_HEADS = 8
IDX_DIM = 64
TOPK_MAX = 256
N_RNN_BLOCKS = 8
CONV_W = 4
LRU_C = 8.0
N_BUCKETS = 32
MAX_DISTANCE = 128
LN_EPS = 1e-5
PAGE_SIZE = 128

LANES = 128
SUBLANES = 8
VMEM_LIMIT = 56 * 1024 * 1024

INT_MIN = -(2 ** 31)
NEG_INF = float("-inf")

TQ = 256
KC = 256
PAGE_RING = 4


def _nt_dot(a, b, precision=None):
    return lax.dot_general(a, b, (((1,), (1,)), ((), ())), precision=precision,
                           preferred_element_type=F32)


def _dot(a, b, precision=None):
    return jnp.dot(a, b, precision=precision, preferred_element_type=F32)


def _t5_bucket(dist):
    dist = jnp.maximum(dist, 0)
    max_exact = N_BUCKETS // 2
    d = jnp.maximum(dist, 1).astype(F32)
    large = max_exact + (jnp.log(d / max_exact) / math.log(MAX_DISTANCE / max_exact)
                         * (N_BUCKETS - max_exact)).astype(I32)
    large = jnp.minimum(large, N_BUCKETS - 1)
    return jnp.where(dist < max_exact, dist, large)


def _bucket_lookup(table, bucket):
    out = jnp.zeros((table.shape[0],) + bucket.shape, F32)
    for b in range(table.shape[1]):
        out = jnp.where(bucket[None] == b, table[:, b].reshape((-1,) + (1,) * bucket.ndim), out)
    return out


def _sortable_key(s):
    bits = lax.bitcast_convert_type(s, I32)
    return jnp.where(bits >= 0, bits, jnp.int32(INT_MIN) - bits)


def _proj_body(x_ref, *refs, plan, precision):
    n_w = len(plan)
    w_refs, o_refs = refs[:n_w], refs[n_w:]
    x = x_ref[...].astype(w_refs[0].dtype)
    oi = 0
    for (channel_major, outs), w_ref in zip(plan, w_refs):
        y = _nt_dot(w_ref[...], x, precision) if channel_major else _nt_dot(x, w_ref[...], precision)
        for lo, hi, scale, dtype in outs:
            part = y[lo:hi, :] if channel_major else y[:, lo:hi]
            if scale != 1.0:
                part = part * scale
            if channel_major:
                o_refs[oi][0] = part.astype(dtype)
            else:
                o_refs[oi][...] = part.astype(dtype)
            oi += 1


def _project(x2d, weights_t, plan, batch, seq, tm, precision=None):
    n, d = x2d.shape
    assert n == batch * seq and seq % tm == 0
    nt = seq // tm
    in_specs = [pl.BlockSpec((tm, d), lambda b, t: (b * nt + t, 0))]
    for w in weights_t:
        in_specs.append(pl.BlockSpec(w.shape, lambda b, t: (0, 0)))
    out_shapes, out_specs = [], []
    for (channel_major, outs) in plan:
        for lo, hi, _, dtype in outs:
            if channel_major:
                out_shapes.append(jax.ShapeDtypeStruct((batch, hi - lo, seq), dtype))
                out_specs.append(pl.BlockSpec((1, hi - lo, tm), lambda b, t: (b, 0, t)))
            else:
                out_shapes.append(jax.ShapeDtypeStruct((n, hi - lo), dtype))
                out_specs.append(pl.BlockSpec((tm, hi - lo), lambda b, t: (b * nt + t, 0)))
    return pl.pallas_call(
        functools.partial(_proj_body, plan=plan, precision=precision),
        grid=(batch, nt),
        in_specs=in_specs,
        out_specs=out_specs,
        out_shape=out_shapes,
        compiler_params=pltpu.CompilerParams(dimension_semantics=("arbitrary", "arbitrary"),
                                             vmem_limit_bytes=VMEM_LIMIT),
        name="in_proj",
    )(x2d, *weights_t)


def _count_rows(keys_ref, nch, pred):
    tq = keys_ref.shape[1]

    def body(j, acc):
        off = pl.multiple_of(j * KC, KC)
        m = pred(keys_ref[pl.ds(off, KC), :], off)
        part = jnp.where(m, 1, 0).astype(I32)
        return acc + jnp.sum(part.reshape(KC // SUBLANES, SUBLANES, tq), axis=0)

    acc = lax.fori_loop(0, nch, body, jnp.zeros((SUBLANES, tq), I32))
    return jnp.sum(acc, axis=0, keepdims=True)


def _select_threshold(keys_ref, nch, topk, idx_bits):
    tq = keys_ref.shape[1]
    c0 = _count_rows(keys_ref, nch, lambda k, off: k >= 0)
    nonneg = c0 >= topk
    prefix = jnp.where(nonneg, 0, INT_MIN).astype(I32)
    cge = jnp.where(nonneg, c0, nch * KC).astype(I32)

    def bit_body(it, carry):
        prefix, cge = carry
        cand = prefix | lax.shift_left(jnp.int32(1), 30 - it)
        c = _count_rows(keys_ref, nch, lambda k, off: k >= cand)
        ok = c >= topk
        return jnp.where(ok, cand, prefix), jnp.where(ok, c, cge)

    thr, cge = lax.fori_loop(0, 31, bit_body, (prefix, cge))
    real = thr > INT_MIN
    need = jnp.logical_and(cge > topk, real)
    j_default = jnp.where(real, jnp.int32(2 ** 30), jnp.int32(-1))

    def tie_fn():
        cgt = _count_rows(keys_ref, nch, lambda k, off: k > thr)
        want = topk - cgt

        def body(it, p):
            cand = p | lax.shift_left(jnp.int32(1), idx_bits - 1 - it)

            def pred(k, off):
                kidx = off + lax.broadcasted_iota(I32, (KC, 1), 0)
                return jnp.logical_and(k == thr, kidx < cand)

            c = _count_rows(keys_ref, nch, pred)
            return jnp.where(c < want, cand, p)

        p = lax.fori_loop(0, idx_bits, body, jnp.zeros((1, tq), I32))
        return jnp.where(need, p, j_default)

    any_need = jnp.max(jnp.where(need, 1, 0)) > 0
    jcut = lax.cond(any_need, tie_fn, lambda: j_default)
    return thr, jcut


def _prompt_attn_body(q_ref, qi_ref, wt_ref, k_ref, ki_ref, vt_ref, bias_ref, o_ref,
                      keys_ref, madd_ref, qm_ref, qim_ref, *, topk, idx_bits):
    i = pl.program_id(1)
    nch = i + 1
    tq = q_ref.shape[0]

    lane_hi = lax.broadcasted_iota(I32, (1, LANES), 1) >= HEAD_DIM
    for h in range(N_HEADS):
        hp, sub = divmod(h, 2)
        keep = lane_hi if sub == 1 else jnp.logical_not(lane_hi)
        sl = slice(hp * LANES, (hp + 1) * LANES)
        qm_ref[h] = jnp.where(keep, q_ref[:, sl], jnp.zeros((), BF16))
        qim_ref[h] = jnp.where(keep, qi_ref[:, sl], jnp.zeros((), BF16))

    wt = wt_ref[0]
    qidx = i * tq + lax.broadcasted_iota(I32, (1, tq), 1)

    def score_body(j, _):
        off = pl.multiple_of(j * KC, KC)
        kc = ki_ref[pl.ds(off, KC), :]
        s = jnp.zeros((KC, tq), F32)
        for h in range(N_IDX_HEADS):
            d = _nt_dot(kc, qim_ref[h])
            s = s + wt[h:h + 1, :] * jnp.maximum(d, 0.0)
        kidx = off + lax.broadcasted_iota(I32, (KC, 1), 0)
        keys_ref[pl.ds(off, KC), :] = jnp.where(kidx <= qidx, _sortable_key(s), INT_MIN)
        return 0

    lax.fori_loop(0, nch, score_body, 0)

    thr, jcut = _select_threshold(keys_ref, nch, topk, idx_bits)

    def madd_body(j, _):
        off = pl.multiple_of(j * KC, KC)
        k = keys_ref[pl.ds(off, KC), :]
        kidx = off + lax.broadcasted_iota(I32, (KC, 1), 0)
        sel = jnp.logical_or(k > thr, jnp.logical_and(k == thr, kidx <= jcut))
        madd_ref[pl.ds(off, KC), :] = jnp.where(sel, 0.0, NEG_INF).astype(F32)
        return 0

    lax.fori_loop(0, nch, madd_body, 0)

    def pair_body(hp, _):
        loff = pl.multiple_of(hp * LANES, LANES)

        def chunk_step(j, carry, bias_slot):
            off = pl.multiple_of(j * KC, KC)
            ksl = k_ref[pl.ds(off, KC), pl.ds(loff, LANES)]
            md = madd_ref[pl.ds(off, KC), :]
            out = []
            for sub in range(2):
                h = 2 * hp + sub
                m, l, acc = carry[sub]
                lg = _nt_dot(ksl, qm_ref[h]) + md
                if bias_slot is not None:
                    lg = lg + bias_ref[bias_slot, h]
                m_new = jnp.maximum(m, jnp.max(lg, axis=0, keepdims=True))
                m_use = jnp.where(m_new == NEG_INF, 0.0, m_new)
                p = jnp.exp(lg - m_use)
                alpha = jnp.exp(m - m_use)
                l_new = alpha * l + jnp.sum(p, axis=0, keepdims=True)
                vsl = vt_ref[0, pl.ds(pl.multiple_of(h * HEAD_DIM, HEAD_DIM), HEAD_DIM), pl.ds(off, KC)]
                acc_new = alpha * acc + _dot(vsl, p.astype(BF16))
                out.append((m_new, l_new, acc_new))
            return tuple(out)

        one = (jnp.full((1, tq), NEG_INF, F32), jnp.zeros((1, tq), F32), jnp.zeros((HEAD_DIM, tq), F32))
        carry = lax.fori_loop(0, jnp.maximum(i - 1, 0), lambda j, c: chunk_step(j, c, None), (one, one))
        carry = lax.cond(i >= 1, lambda c: chunk_step(i - 1, c, 1), lambda c: c, carry)
        carry = chunk_step(i, carry, 0)
        for sub in range(2):
            m, l, acc = carry[sub]
            row = pl.multiple_of((2 * hp + sub) * HEAD_DIM, HEAD_DIM)
            o_ref[pl.ds(row, HEAD_DIM), :] = (acc / l).astype(o_ref.dtype)
        return 0

    lax.fori_loop(0, N_HEADS // 2, pair_body, 0)


def _prompt_attention(qb, qib, wit, kb, ki2b, vt, bias_t, batch, seq):
    assert seq % TQ == 0 and TQ == KC
    nq = seq // TQ
    topk = min(TOPK_MAX, seq // 4)
    idx_bits = max(1, (seq - 1).bit_length())
    n = batch * seq
    return pl.pallas_call(
        functools.partial(_prompt_attn_body, topk=topk, idx_bits=idx_bits),
        grid=(batch, nq),
        in_specs=[
            pl.BlockSpec((TQ, D_ATT), lambda b, i: (b * nq + i, 0)),
            pl.BlockSpec((TQ, N_IDX_HEADS * IDX_DIM), lambda b, i: (b * nq + i, 0)),
            pl.BlockSpec((1, N_IDX_HEADS, TQ), lambda b, i: (b, 0, i)),
            pl.BlockSpec((seq, D_ATT), lambda b, i: (b, 0)),
            pl.BlockSpec((seq, 2 * IDX_DIM), lambda b, i: (b, 0)),
            pl.BlockSpec((1, D_ATT, seq), lambda b, i: (b, 0, 0)),
            pl.BlockSpec(bias_t.shape, lambda b, i: (0, 0, 0, 0)),
        ],
        out_specs=pl.BlockSpec((D_ATT, TQ), lambda b, i: (0, b * nq + i)),
        out_shape=jax.ShapeDtypeStruct((D_ATT, n), BF16),
        scratch_shapes=[
            pltpu.VMEM((seq, TQ), I32),
            pltpu.VMEM((seq, TQ), F32),
            pltpu.VMEM((N_HEADS, TQ, LANES), BF16),
            pltpu.VMEM((N_IDX_HEADS, TQ, LANES), BF16),
        ],
        compiler_params=pltpu.CompilerParams(dimension_semantics=("arbitrary", "arbitrary"),
                                             vmem_limit_bytes=VMEM_LIMIT),
        name="prompt_attention",
    )(qb, qib, wit, kb, ki2b, vt, bias_t)


def _prompt_bias_tiles(rel_bias):
    shifted = (rel_bias - rel_bias[N_BUCKETS - 1:N_BUCKETS]).T.astype(F32)
    c = jnp.arange(KC, dtype=I32)[:, None]
    r = jnp.arange(TQ, dtype=I32)[None, :]
    tiles = []
    for base in (0, TQ):
        dist = base + r - c
        bucket = jnp.where(dist >= 0, _t5_bucket(dist), N_BUCKETS - 1)
        tiles.append(_bucket_lookup(shifted, bucket))
    return jnp.stack(tiles)


def _lru_gates(xc, wa_ref, ba_ref, wx_ref, bx_ref, lam_ref, precision):
    xm = xc.astype(wa_ref.dtype)
    r = jax.nn.sigmoid(_dot(xm, wa_ref[...], precision) + ba_ref[...])
    g = jax.nn.sigmoid(_dot(xm, wx_ref[...], precision) + bx_ref[...])
    log_a = -LRU_C * r * jax.nn.softplus(-lam_ref[...])
    a = jnp.exp(log_a)
    u = jnp.sqrt(-jnp.tanh(log_a) * (a * a + 1.0)) * g * xc
    return a, u


def _rglru_prompt_body(xr_ref, gr_ref, cw_ref, cb_ref, wa_ref, ba_ref, wx_ref, bx_ref, lam_ref,
                       rnn_ref, hlast_ref, ext_ref, hc_ref):
    tt = xr_ref.shape[0]

    @pl.when(pl.program_id(1) == 0)
    def _():
        ext_ref[0:SUBLANES, :] = jnp.zeros((SUBLANES, ext_ref.shape[1]), F32)
        hc_ref[...] = jnp.zeros_like(hc_ref)

    x = xr_ref[...]
    ext_ref[SUBLANES:SUBLANES + tt, :] = x
    cw = cw_ref[...]
    xc = cb_ref[...] + ext_ref[SUBLANES - 3:SUBLANES - 3 + tt, :] * cw[0:1]
    for j in range(1, CONV_W):
        s = CONV_W - 1 - j
        xc = xc + ext_ref[SUBLANES - s:SUBLANES - s + tt, :] * cw[j:j + 1]
    ext_ref[0:SUBLANES, :] = x[tt - SUBLANES:tt]

    a, u = _lru_gates(xc, wa_ref, ba_ref, wx_ref, bx_ref, lam_ref, None)
    row = lax.broadcasted_iota(I32, (tt, 1), 0)
    d = 1
    while d < tt:
        keep = row >= d
        a_sh = jnp.where(keep, pltpu.roll(a, d, 0), 1.0)
        u_sh = jnp.where(keep, pltpu.roll(u, d, 0), 0.0)
        u = u + a * u_sh
        a = a * a_sh
        d *= 2
    h = u + a * hc_ref[...]
    h_last = h[tt - 1:tt]
    hc_ref[...] = h_last
    hlast_ref[0] = h_last
    rnn_ref[...] = (h * jax.nn.gelu(gr_ref[...])).astype(rnn_ref.dtype)


def _rglru_prompt(xr, gr, cw, cb, wa_bd, ba, wx_bd, bx, lam, batch, seq, tt):
    n, dr = xr.shape
    nt = seq // tt
    full = lambda shape: pl.BlockSpec(shape, lambda b, t: (0,) * len(shape))
    return pl.pallas_call(
        _rglru_prompt_body,
        grid=(batch, nt),
        in_specs=[
            pl.BlockSpec((tt, dr), lambda b, t: (b * nt + t, 0)),
            pl.BlockSpec((tt, dr), lambda b, t: (b * nt + t, 0)),
            full(cw.shape), full(cb.shape), full(wa_bd.shape), full(ba.shape),
            full(wx_bd.shape), full(bx.shape), full(lam.shape),
        ],
        out_specs=[
            pl.BlockSpec((tt, dr), lambda b, t: (b * nt + t, 0)),
            pl.BlockSpec((1, 1, dr), lambda b, t: (b, 0, 0)),
        ],
        out_shape=[jax.ShapeDtypeStruct((n, dr), BF16), jax.ShapeDtypeStruct((batch, 1, dr), F32)],
        scratch_shapes=[pltpu.VMEM((tt + SUBLANES, dr), F32), pltpu.VMEM((1, dr), F32)],
        compiler_params=pltpu.CompilerParams(dimension_semantics=("arbitrary", "arbitrary"),
                                             vmem_limit_bytes=VMEM_LIMIT),
        name="rglru_prompt",
    )(xr, gr, cw, cb, wa_bd, ba, wx_bd, bx, lam)


def _rglru_sample_body(xr_ref, gr_ref, cp_ref, hp_ref, cw_ref, cb_ref, wa_ref, ba_ref, wx_ref, bx_ref,
                       lam_ref, rnn_ref, h_ref):
    cw = cw_ref[...]
    xc = cb_ref[...] + cp_ref[0] * cw[0:1]
    for j in range(1, CONV_W - 1):
        xc = xc + cp_ref[j] * cw[j:j + 1]
    xc = xc + xr_ref[...] * cw[CONV_W - 1:CONV_W]
    a, u = _lru_gates(xc, wa_ref, ba_ref, wx_ref, bx_ref, lam_ref, lax.Precision.HIGHEST)
    h = u + a * hp_ref[...]
    h_ref[...] = h
    rnn_ref[...] = h * jax.nn.gelu(gr_ref[...])


def _rglru_sample(xr, gr, conv_prev_t, h_prev, cw, cb, wa_bd, ba, wx_bd, bx, lam):
    n, dr = xr.shape
    return pl.pallas_call(
        _rglru_sample_body,
        out_shape=[jax.ShapeDtypeStruct((n, dr), F32), jax.ShapeDtypeStruct((n, dr), F32)],
        compiler_params=pltpu.CompilerParams(vmem_limit_bytes=VMEM_LIMIT),
        name="rglru_sample",
    )(xr, gr, conv_prev_t, h_prev, cw, cb, wa_bd, ba, wx_bd, bx, lam)


def _layer_norm(x, g, b):
    mu = jnp.mean(x, axis=-1, keepdims=True)
    xc = x - mu
    var = jnp.mean(jnp.square(xc), axis=-1, keepdims=True)
    return xc * lax.rsqrt(var + LN_EPS) * g + b


def _mlp_body(attn_ref, rnn_ref, x_ref, woa_ref, wor_ref, g1_ref, b1_ref, wup_ref, bup_ref,
              wdn_ref, bdn_ref, g2_ref, b2_ref, y_ref, *, alpha, attn_channel_major, ff_chunk):
    cdt = woa_ref.dtype
    if attn_channel_major:
        mix = lax.dot_general(attn_ref[...].astype(cdt), woa_ref[...], (((0,), (0,)), ((), ())),
                              preferred_element_type=F32)
    else:
        mix = _dot(attn_ref[...].astype(cdt), woa_ref[...])
    mix = mix + _dot(rnn_ref[...].astype(cdt), wor_ref[...])
    x1 = _layer_norm(alpha * x_ref[...] + mix, g1_ref[...], b1_ref[...])
    x1c = x1.astype(cdt)
    d_ff = wup_ref.shape[1]
    y = None
    for c in range(0, d_ff, ff_chunk):
        hid = _dot(x1c, wup_ref[:, c:c + ff_chunk]) + bup_ref[:, c:c + ff_chunk]
        hid = jnp.square(jnp.maximum(hid, 0.0)).astype(cdt)
        part = _dot(hid, wdn_ref[c:c + ff_chunk, :])
        y = part if y is None else y + part
    y_ref[...] = _layer_norm(alpha * x1 + y + bdn_ref[...], g2_ref[...], b2_ref[...])


def _out_mlp(attn, rnn, x2d, woa, wor, g1, b1, wup, bup, wdn, bdn, g2, b2, *, alpha, tm,
             attn_channel_major):
    n, dm = x2d.shape
    da = woa.shape[0]
    full = lambda a: pl.BlockSpec(a.shape, lambda i: (0,) * a.ndim)
    attn_spec = (pl.BlockSpec((da, tm), lambda i: (0, i)) if attn_channel_major
                 else pl.BlockSpec((tm, da), lambda i: (i, 0)))
    consts = (woa, wor, g1, b1, wup, bup, wdn, bdn, g2, b2)
    return pl.pallas_call(
        functools.partial(_mlp_body, alpha=alpha, attn_channel_major=attn_channel_major, ff_chunk=1024),
        grid=(n // tm,),
        in_specs=[attn_spec, pl.BlockSpec((tm, rnn.shape[1]), lambda i: (i, 0)),
                  pl.BlockSpec((tm, dm), lambda i: (i, 0))] + [full(a) for a in consts],
        out_specs=pl.BlockSpec((tm, dm), lambda i: (i, 0)),
        out_shape=jax.ShapeDtypeStruct((n, dm), F32),
        compiler_params=pltpu.CompilerParams(dimension_semantics=("arbitrary",),
                                             vmem_limit_bytes=VMEM_LIMIT),
        name="out_mlp",
    )(attn, rnn, x2d, *consts)


def _sample_keys_body(pt_ref, qi_ref, w_ref, kin_ref, cki_ref, keys_ref, knew_ref, kibuf, sem,
                      *, layer):
    b = pl.program_id(0)
    nb = pl.num_programs(0)
    n_pages = kibuf.shape[1]

    def page_copy(bb, slot, p):
        return pltpu.make_async_copy(cki_ref.at[layer, pt_ref[bb, p]], kibuf.at[slot, p], sem.at[slot])

    def issue_all(bb, slot):
        def body(p, _):
            page_copy(bb, slot, p).start()
            return 0
        lax.fori_loop(0, n_pages, body, 0)

    slot = lax.rem(b, 2)

    @pl.when(b == 0)
    def _():
        issue_all(b, slot)

    @pl.when(b + 1 < nb)
    def _():
        issue_all(b + 1, 1 - slot)

    def wait_body(p, _):
        page_copy(b, slot, p).wait()
        return 0

    lax.fori_loop(0, n_pages, wait_body, 0)

    qi16 = qi_ref[0].astype(BF16)
    w = w_ref[0]
    group = 4

    def score_body(g, _):
        for u in range(group):
            p = g * group + u
            d = _dot(qi16, kibuf[slot, p].astype(BF16))
            key = _sortable_key(jnp.sum(w * jnp.maximum(d, 0.0), axis=0, keepdims=True))
            keys_ref[0, pl.ds(p, 1), :] = key
        return 0

    lax.fori_loop(0, n_pages // group, score_body, 0)

    dn = jnp.sum(qi16.astype(F32) * kin_ref[0].astype(BF16).astype(F32), axis=1, keepdims=True)
    key_new = _sortable_key(jnp.sum(w * jnp.maximum(dn, 0.0), axis=0, keepdims=True))
    knew_ref[0] = jnp.broadcast_to(key_new, (1, LANES))


def _sample_keys(page_table, qi_s, wi_col, ki_new, cache_ki_t, layer):
    bd, n_pages = page_table.shape
    assert n_pages % 4 == 0
    grid_spec = pltpu.PrefetchScalarGridSpec(
        num_scalar_prefetch=1,
        grid=(bd,),
        in_specs=[
            pl.BlockSpec((1, N_IDX_HEADS, IDX_DIM), lambda b, pt: (b, 0, 0)),
            pl.BlockSpec((1, N_IDX_HEADS, 1), lambda b, pt: (b, 0, 0)),
            pl.BlockSpec((1, 1, IDX_DIM), lambda b, pt: (b, 0, 0)),
            pl.BlockSpec(memory_space=pl.ANY),
        ],
        out_specs=[
            pl.BlockSpec((1, n_pages, PAGE_SIZE), lambda b, pt: (b, 0, 0)),
            pl.BlockSpec((1, 1, LANES), lambda b, pt: (b, 0, 0)),
        ],
        scratch_shapes=[
            pltpu.VMEM((2, n_pages, IDX_DIM, PAGE_SIZE), F32),
            pltpu.SemaphoreType.DMA((2,)),
        ],
    )
    return pl.pallas_call(
        functools.partial(_sample_keys_body, layer=layer),
        grid_spec=grid_spec,
        out_shape=[jax.ShapeDtypeStruct((bd, n_pages, PAGE_SIZE), I32),
                   jax.ShapeDtypeStruct((bd, 1, LANES), I32)],
        compiler_params=pltpu.CompilerParams(dimension_semantics=("arbitrary",),
                                             vmem_limit_bytes=VMEM_LIMIT),
        name="sample_keys",
    )(page_table, qi_s, wi_col, ki_new, cache_ki_t)


def _sample_mask_body(keys_ref, knew_ref, madd_ref, maddn_ref, *, topk, idx_bits):
    keys = keys_ref[...]
    key_new = knew_ref[:, :, 0:1]
    n_past = keys.shape[1] * keys.shape[2]
    kidx = (lax.broadcasted_iota(I32, keys.shape, 1) * PAGE_SIZE
            + lax.broadcasted_iota(I32, keys.shape, 2))

    def count(pred_past, pred_new):
        c = jnp.sum(jnp.where(pred_past, 1, 0).astype(I32), axis=1, keepdims=True)
        return jnp.sum(c, axis=2, keepdims=True) + jnp.where(pred_new, 1, 0).astype(I32)

    c0 = count(keys >= 0, key_new >= 0)
    nonneg = c0 >= topk
    prefix = jnp.where(nonneg, 0, INT_MIN).astype(I32)
    cge = jnp.where(nonneg, c0, n_past + 1).astype(I32)

    def bit_body(it, carry):
        prefix, cge = carry
        cand = prefix | lax.shift_left(jnp.int32(1), 30 - it)
        c = count(keys >= cand, key_new >= cand)
        ok = c >= topk
        return jnp.where(ok, cand, prefix), jnp.where(ok, c, cge)

    thr, cge = lax.fori_loop(0, 31, bit_body, (prefix, cge))
    real = thr > INT_MIN
    need = jnp.logical_and(cge > topk, real)
    j_default = jnp.where(real, jnp.int32(2 ** 30), jnp.int32(-1))

    def tie_fn():
        want = topk - count(keys > thr, key_new > thr)

        def body(it, p):
            cand = p | lax.shift_left(jnp.int32(1), idx_bits - 1 - it)
            c = count(jnp.logical_and(keys == thr, kidx < cand),
                      jnp.logical_and(key_new == thr, n_past < cand))
            return jnp.where(c < want, cand, p)

        p = lax.fori_loop(0, idx_bits, body, jnp.zeros(thr.shape, I32))
        return jnp.where(need, p, j_default)

    jcut = lax.cond(jnp.max(jnp.where(need, 1, 0)) > 0, tie_fn, lambda: j_default)
    sel = jnp.logical_or(keys > thr, jnp.logical_and(keys == thr, kidx <= jcut))
    sel_new = jnp.logical_or(key_new > thr, jnp.logical_and(key_new == thr, n_past <= jcut))
    madd_ref[...] = jnp.where(sel, 0.0, NEG_INF).astype(F32)
    maddn_ref[...] = jnp.broadcast_to(jnp.where(sel_new, 0.0, NEG_INF).astype(F32), maddn_ref.shape)


def _sample_mask(keys, key_new, topk):
    bd, n_pages, _ = keys.shape
    idx_bits = max(1, (n_pages * PAGE_SIZE).bit_length())
    return pl.pallas_call(
        functools.partial(_sample_mask_body, topk=topk, idx_bits=idx_bits),
        out_shape=[jax.ShapeDtypeStruct(keys.shape, F32), jax.ShapeDtypeStruct(key_new.shape, F32)],
        compiler_params=pltpu.CompilerParams(vmem_limit_bytes=VMEM_LIMIT),
        name="sample_mask",
    )(keys, key_new)


def _sample_attend_body(pt_ref, qb_ref, q_ref, kn_ref, vn_ref, madd_ref, maddn_ref, btab_ref,
                        ck_ref, cv_ref, o_ref, kring, vring, lg_ref, acc_ref, ksem, vsem, *, layer):
    b = pl.program_id(0)
    n_pages = madd_ref.shape[1]
    ring = kring.shape[0]

    def k_copy(p, slot):
        return pltpu.make_async_copy(ck_ref.at[layer, pt_ref[b, p]], kring.at[slot], ksem.at[slot])

    def v_copy(p, slot):
        return pltpu.make_async_copy(cv_ref.at[layer, pt_ref[b, p]], vring.at[slot], vsem.at[slot])

    for u in range(ring):
        k_copy(u, u).start()
    for u in range(ring):
        v_copy(u, u).start()

    def k_body(g, _):
        for u in range(ring):
            p = g * ring + u
            k_copy(p, u).wait()
            mrow = madd_ref[0, pl.ds(p, 1), :]
            is_last = p == n_pages - 1
            for h in range(N_HEADS):
                lg = jnp.sum(kring[u, h] * qb_ref[0, h], axis=0, keepdims=True) + mrow
                lg_ref[h, pl.ds(p, 1), :] = lg + jnp.where(is_last, btab_ref[h:h + 1, 0:PAGE_SIZE], 0.0)

            @pl.when(p + ring < n_pages)
            def _():
                k_copy(p + ring, u).start()
        return 0

    lax.fori_loop(0, n_pages // ring, k_body, 0)

    def total(x, op):
        return op(op(x, axis=0, keepdims=True), axis=1, keepdims=True)

    lg_new = (jnp.sum(q_ref[0] * kn_ref[0], axis=1, keepdims=True)
              + btab_ref[:, PAGE_SIZE:PAGE_SIZE + 1] + maddn_ref[0][:, 0:1])
    p_new, l = [], []
    for h in range(N_HEADS):
        lgh = lg_ref[h]
        mh = jnp.maximum(total(lgh, jnp.max), lg_new[h:h + 1])
        prh = jnp.exp(lgh - mh)
        lg_ref[h] = prh
        p_new.append(jnp.exp(lg_new[h:h + 1] - mh))
        l.append(total(prh, jnp.sum) + p_new[h])

    acc_ref[...] = jnp.zeros_like(acc_ref)

    def v_body(g, _):
        for u in range(ring):
            p = g * ring + u
            v_copy(p, u).wait()
            for h in range(N_HEADS):
                acc_ref[h] += vring[u, h] * lg_ref[h, pl.ds(p, 1), :]

            @pl.when(p + ring < n_pages)
            def _():
                v_copy(p + ring, u).start()
        return 0

    lax.fori_loop(0, n_pages // ring, v_body, 0)

    for h in range(N_HEADS):
        past = jnp.sum(acc_ref[h].T, axis=0, keepdims=True)
        o_ref[0, h:h + 1, :] = (past + p_new[h] * vn_ref[0, h:h + 1, :]) / l[h]


def _sample_attend(page_table, q_lanes, q_s, k_new, v_new, madd, madd_new, btab, cache_k_t, cache_v_t,
                   layer):
    bd, n_pages = page_table.shape
    assert n_pages % PAGE_RING == 0
    blk3 = lambda b, pt: (b, 0, 0)
    grid_spec = pltpu.PrefetchScalarGridSpec(
        num_scalar_prefetch=1,
        grid=(bd,),
        in_specs=[
            pl.BlockSpec((1, N_HEADS, HEAD_DIM, PAGE_SIZE), lambda b, pt: (b, 0, 0, 0)),
            pl.BlockSpec((1, N_HEADS, HEAD_DIM), blk3),
            pl.BlockSpec((1, N_HEADS, HEAD_DIM), blk3),
            pl.BlockSpec((1, N_HEADS, HEAD_DIM), blk3),
            pl.BlockSpec((1, n_pages, PAGE_SIZE), blk3),
            pl.BlockSpec((1, 1, LANES), blk3),
            pl.BlockSpec(btab.shape, lambda b, pt: (0, 0)),
            pl.BlockSpec(memory_space=pl.ANY),
            pl.BlockSpec(memory_space=pl.ANY),
        ],
        out_specs=pl.BlockSpec((1, N_HEADS, HEAD_DIM), blk3),
        scratch_shapes=[
            pltpu.VMEM((PAGE_RING, N_HEADS, HEAD_DIM, PAGE_SIZE), F32),
            pltpu.VMEM((PAGE_RING, N_HEADS, HEAD_DIM, PAGE_SIZE), F32),
            pltpu.VMEM((N_HEADS, n_pages, PAGE_SIZE), F32),
            pltpu.VMEM((N_HEADS, HEAD_DIM, PAGE_SIZE), F32),
            pltpu.SemaphoreType.DMA((PAGE_RING,)),
            pltpu.SemaphoreType.DMA((PAGE_RING,)),
        ],
    )
    return pl.pallas_call(
        functools.partial(_sample_attend_body, layer=layer),
        grid_spec=grid_spec,
        out_shape=jax.ShapeDtypeStruct((bd, N_HEADS, HEAD_DIM), F32),
        compiler_params=pltpu.CompilerParams(dimension_semantics=("arbitrary",),
                                             vmem_limit_bytes=VMEM_LIMIT),
        name="sample_attend",
    )(page_table, q_lanes, q_s, k_new, v_new, madd, madd_new, btab, cache_k_t, cache_v_t)


def _block_diag(w):
    nb, c, d = w.shape
    eye = jnp.eye(nb, dtype=w.dtype)
    return (eye[:, None, :, None] * w[:, :, None, :]).reshape(nb * c, nb * d)


def kernel(x_prompt, x_sample, cache_k, cache_v, cache_k_idx, state_h, state_conv, page_table,
           rel_bias, w_in, conv_w, conv_b, w_a, b_a, w_x, b_x, lru_lambda, w_out,
           ln1_g, ln1_b, w_up, b_up, w_down, b_down, ln2_g, ln2_b):
    depth = w_in.shape[0]
    batch, seq, dm = x_prompt.shape
    bd, dec_seq, _ = x_sample.shape
    assert dec_seq == 1, "the decode pass handles one new token per sequence"
    d_rnn = dm - D_ATT
    alpha = (2 * depth) ** 0.25
    n_pages = page_table.shape[1]
    n_past = n_pages * PAGE_SIZE
    topk_s = min(TOPK_MAX, (n_past + dec_seq) // 4)
    row = lambda a: a.reshape(1, -1)
    c16 = lambda a: a.astype(BF16)

    sizes = (D_ATT, D_ATT, D_ATT, N_IDX_HEADS * IDX_DIM, IDX_DIM, N_IDX_HEADS, d_rnn, d_rnn)
    offs = [0]
    for s in sizes:
        offs.append(offs[-1] + s)

    w_in_t = jnp.swapaxes(w_in, 1, 2)
    cache_ki_t = jnp.swapaxes(cache_k_idx, 2, 3)
    cache_k_t = jnp.transpose(cache_k, (0, 1, 3, 4, 2))
    cache_v_t = jnp.transpose(cache_v, (0, 1, 3, 4, 2))

    bias_t = _prompt_bias_tiles(rel_bias)
    shifted = (rel_bias - rel_bias[N_BUCKETS - 1:N_BUCKETS]).T.astype(F32)
    btab = jnp.concatenate([
        _bucket_lookup(shifted, _t5_bucket(PAGE_SIZE - jnp.arange(PAGE_SIZE, dtype=I32))),
        _bucket_lookup(shifted, _t5_bucket(jnp.zeros((PAGE_SIZE,), I32)))], axis=1)

    yp = x_prompt.reshape(batch * seq, dm)
    ys = x_sample.reshape(bd, dm)
    outs = [[] for _ in range(10)]
    for layer in range(depth):
        wt = w_in_t[layer]
        wq, wk, wv, wqi, wki, wwi, wxr, wgr = [wt[offs[j]:offs[j + 1]] for j in range(8)]
        cw, cb, lam = conv_w[layer], row(conv_b[layer]), row(lru_lambda[layer])
        wa_bd, wx_bd = _block_diag(w_a[layer]), _block_diag(w_x[layer])
        ba, bx = row(b_a[layer]), row(b_x[layer])
        woa, wor = c16(w_out[layer][:D_ATT]), c16(w_out[layer][D_ATT:])
        mlp_consts = (woa, wor, row(ln1_g[layer]), row(ln1_b[layer]), c16(w_up[layer]), row(b_up[layer]),
                      c16(w_down[layer]), row(b_down[layer]), row(ln2_g[layer]), row(ln2_b[layer]))

        p_weights = [c16(wq), c16(wk), c16(wk), c16(wv), c16(wqi), c16(jnp.concatenate([wki, wki], 0)),
                     c16(wki), c16(wwi), c16(wxr), c16(wgr)]
        p_plan = (
            (False, ((0, D_ATT, HEAD_DIM ** -0.5, BF16),)),
            (False, ((0, D_ATT, 1.0, BF16),)),
            (True, ((0, D_ATT, 1.0, F32),)),
            (True, ((0, D_ATT, 1.0, F32), (0, D_ATT, 1.0, BF16))),
            (False, ((0, N_IDX_HEADS * IDX_DIM, 1.0, BF16),)),
            (False, ((0, 2 * IDX_DIM, 1.0, BF16),)),
            (True, ((0, IDX_DIM, 1.0, F32),)),
            (True, ((0, N_IDX_HEADS, 1.0, F32),)),
            (False, ((0, d_rnn, 1.0, F32),)),
            (False, ((0, d_rnn, 1.0, F32),)),
        )
        qb, kb, k_t, v_t, vtb, qib, ki2b, ki_t, wit, xr_p, gr_p = _project(
            yp, p_weights, p_plan, batch, seq, tm=512)
        attn_t = _prompt_attention(qb, qib, wit, kb, ki2b, vtb, bias_t, batch, seq)
        rnn_p, h_p = _rglru_prompt(xr_p, gr_p, cw, cb, c16(wa_bd), ba, c16(wx_bd), bx, lam,
                                   batch, seq, tt=512)
        conv_p = xr_p.reshape(batch, seq, d_rnn)[:, seq - (CONV_W - 1):]
        yp = _out_mlp(attn_t, rnn_p, yp, *mlp_consts, alpha=alpha, tm=512, attn_channel_major=True)

        pad_wi = jnp.pad(wwi, ((0, LANES - N_IDX_HEADS), (0, 0)))
        s_weights = [wq, wk, wv, wqi, wki, pad_wi, wxr, wgr]
        s_plan = (
            (False, ((0, D_ATT, HEAD_DIM ** -0.5, F32),)),
            (False, ((0, D_ATT, 1.0, F32),)),
            (False, ((0, D_ATT, 1.0, F32),)),
            (False, ((0, N_IDX_HEADS * IDX_DIM, 1.0, F32),)),
            (False, ((0, IDX_DIM, 1.0, F32),)),
            (False, ((0, LANES, 1.0, F32),)),
            (False, ((0, d_rnn, 1.0, F32),)),
            (False, ((0, d_rnn, 1.0, F32),)),
        )
        q_s, k_s, v_s, qi_s, ki_s, wi_s, xr_s, gr_s = _project(ys, s_weights, s_plan, 1, bd, tm=bd,
                                                               precision=lax.Precision.HIGHEST)
        keys, key_new = _sample_keys(page_table, qi_s.reshape(bd, N_IDX_HEADS, IDX_DIM),
                                     wi_s[:, :N_IDX_HEADS].reshape(bd, N_IDX_HEADS, 1),
                                     ki_s.reshape(bd, 1, IDX_DIM), cache_ki_t, layer)
        madd, madd_new = _sample_mask(keys, key_new, topk_s)
        heads = lambda a: a.reshape(bd, N_HEADS, HEAD_DIM)
        q_lanes = jnp.broadcast_to(heads(q_s)[..., None], (bd, N_HEADS, HEAD_DIM, PAGE_SIZE))
        attn_s = _sample_attend(page_table, q_lanes, heads(q_s), heads(k_s), heads(v_s), madd, madd_new,
                                btab, cache_k_t, cache_v_t, layer)
        conv_prev = state_conv[layer]
        rnn_s, h_s = _rglru_sample(xr_s, gr_s, jnp.swapaxes(conv_prev, 0, 1), state_h[layer], cw, cb,
                                   wa_bd, ba, wx_bd, bx, lam)
        conv_s = jnp.concatenate([conv_prev[:, 1:], xr_s[:, None, :]], axis=1)
        ys = _out_mlp(attn_s.reshape(bd, D_ATT), rnn_s, ys, *mlp_consts, alpha=alpha, tm=bd,
                      attn_channel_major=False)

        per_head_t = lambda a: jnp.transpose(a.reshape(batch, N_HEADS, HEAD_DIM, seq), (0, 3, 1, 2))
        layer_outs = (
            per_head_t(k_t), per_head_t(v_t), jnp.swapaxes(ki_t, 1, 2), h_p.reshape(batch, d_rnn), conv_p,
            k_s.reshape(bd, dec_seq, N_HEADS, HEAD_DIM), v_s.reshape(bd, dec_seq, N_HEADS, HEAD_DIM),
            ki_s.reshape(bd, dec_seq, IDX_DIM), h_s, conv_s,
        )
        for lst, a in zip(outs, layer_outs):
            lst.append(a)

    stacked = [jnp.stack(lst) for lst in outs]
    return (yp.reshape(batch, seq, dm), ys.reshape(bd, dec_seq, dm), *stacked)
```

```python
import functools
import math

import jax
import jax.numpy as jnp
from jax import lax
from jax.experimental import pallas as pl
from jax.experimental.pallas import tpu as pltpu

F32, BF16, I32 = jnp.float32, jnp.bfloat16, jnp.int32

N_HEADS = 8
HEAD_DIM = 64
D_ATT = N_HEADS * HEAD_DIM
N_IDX_HEADS = 8
IDX_DIM = 64
TOPK_MAX = 256
N_RNN_BLOCKS = 8
CONV_W = 4
LRU_C = 8.0
N_BUCKETS = 32
MAX_DISTANCE = 128
LN_EPS = 1e-5
PAGE_SIZE = 128

LANES = 128
SUBLANES = 8
VMEM_LIMIT = 56 * 1024 * 1024

INT_MIN = -(2 ** 31)
NEG_INF = float("-inf")

TQ = 256
KC = 256
KB = 1024
PAGE_RING = 32


def _nt_dot(a, b, precision=None):
    return lax.dot_general(a, b, (((1,), (1,)), ((), ())), precision=precision,
                           preferred_element_type=F32)


def _dot(a, b, precision=None):
    return jnp.dot(a, b, precision=precision, preferred_element_type=F32)


def _t5_bucket(dist):
    dist = jnp.maximum(dist, 0)
    max_exact = N_BUCKETS // 2
    d = jnp.maximum(dist, 1).astype(F32)
    large = max_exact + (jnp.log(d / max_exact) / math.log(MAX_DISTANCE / max_exact)
                         * (N_BUCKETS - max_exact)).astype(I32)
    large = jnp.minimum(large, N_BUCKETS - 1)
    return jnp.where(dist < max_exact, dist, large)


def _bucket_lookup(table, bucket):
    out = jnp.zeros((table.shape[0],) + bucket.shape, F32)
    for b in range(table.shape[1]):
        out = jnp.where(bucket[None] == b, table[:, b].reshape((-1,) + (1,) * bucket.ndim), out)
    return out


def _sortable_key(s):
    bits = lax.bitcast_convert_type(s, I32)
    return jnp.where(bits >= 0, bits, jnp.int32(INT_MIN) - bits)


def _proj_body(x_ref, *refs, plan, precision):
    n_w = len(plan)
    w_refs, o_refs = refs[:n_w], refs[n_w:]
    x = x_ref[...].astype(w_refs[0].dtype)
    oi = 0
    for (channel_major, outs), w_ref in zip(plan, w_refs):
        y = _nt_dot(w_ref[...], x, precision) if channel_major else _nt_dot(x, w_ref[...], precision)
        for lo, hi, scale, dtype in outs:
            part = y[lo:hi, :] if channel_major else y[:, lo:hi]
            if scale != 1.0:
                part = part * scale
            if channel_major:
                o_refs[oi][0] = part.astype(dtype)
            else:
                o_refs[oi][...] = part.astype(dtype)
            oi += 1


def _project(x2d, weights_t, plan, batch, seq, tm, precision=None):
    n, d = x2d.shape
    assert n == batch * seq and seq % tm == 0
    nt = seq // tm
    in_specs = [pl.BlockSpec((tm, d), lambda b, t: (b * nt + t, 0))]
    for w in weights_t:
        in_specs.append(pl.BlockSpec(w.shape, lambda b, t: (0, 0)))
    out_shapes, out_specs = [], []
    for (channel_major, outs) in plan:
        for lo, hi, _, dtype in outs:
            if channel_major:
                out_shapes.append(jax.ShapeDtypeStruct((batch, hi - lo, seq), dtype))
                out_specs.append(pl.BlockSpec((1, hi - lo, tm), lambda b, t: (b, 0, t)))
            else:
                out_shapes.append(jax.ShapeDtypeStruct((n, hi - lo), dtype))
                out_specs.append(pl.BlockSpec((tm, hi - lo), lambda b, t: (b * nt + t, 0)))
    return pl.pallas_call(
        functools.partial(_proj_body, plan=plan, precision=precision),
        grid=(batch, nt),
        in_specs=in_specs,
        out_specs=out_specs,
        out_shape=out_shapes,
        compiler_params=pltpu.CompilerParams(dimension_semantics=("arbitrary", "arbitrary"),
                                             vmem_limit_bytes=VMEM_LIMIT),
        name="in_proj",
    )(x2d, *weights_t)


def _count_rows(keys_ref, nbig, pred):
    tq = keys_ref.shape[1]

    def body(j, acc):
        off = pl.multiple_of(j * KB, KB)
        m = pred(keys_ref[pl.ds(off, KB), :], off)
        part = jnp.where(m, 1, 0).astype(I32)
        return acc + jnp.sum(part.reshape(KB // SUBLANES, SUBLANES, tq), axis=0)

    acc = lax.fori_loop(0, nbig, body, jnp.zeros((SUBLANES, tq), I32))
    return jnp.sum(acc, axis=0, keepdims=True)


def _select_threshold(keys_ref, nch, topk, idx_bits):
    tq = keys_ref.shape[1]
    c0 = _count_rows(keys_ref, nch, lambda k, off: k >= 0)
    nonneg = c0 >= topk
    prefix = jnp.where(nonneg, 0, INT_MIN).astype(I32)
    cge = jnp.where(nonneg, c0, nch * KB).astype(I32)

    def bit_body(it, carry):
        prefix, cge = carry
        cand = prefix | lax.shift_left(jnp.int32(1), 30 - it)
        c = _count_rows(keys_ref, nch, lambda k, off: k >= cand)
        ok = c >= topk
        return jnp.where(ok, cand, prefix), jnp.where(ok, c, cge)

    thr, cge = lax.fori_loop(0, 31, bit_body, (prefix, cge))
    real = thr > INT_MIN
    need = jnp.logical_and(cge > topk, real)
    j_default = jnp.where(real, jnp.int32(2 ** 30), jnp.int32(-1))

    def tie_fn():
        cgt = _count_rows(keys_ref, nch, lambda k, off: k > thr)
        want = topk - cgt

        def body(it, p):
            cand = p | lax.shift_left(jnp.int32(1), idx_bits - 1 - it)

            def pred(k, off):
                kidx = off + lax.broadcasted_iota(I32, (KB, 1), 0)
                return jnp.logical_and(k == thr, kidx < cand)

            c = _count_rows(keys_ref, nch, pred)
            return jnp.where(c < want, cand, p)

        p = lax.fori_loop(0, idx_bits, body, jnp.zeros((1, tq), I32))
        return jnp.where(need, p, j_default)

    any_need = jnp.max(jnp.where(need, 1, 0)) > 0
    jcut = lax.cond(any_need, tie_fn, lambda: j_default)
    return thr, jcut


def _prompt_attn_body(q_ref, qi_ref, wt_ref, k_ref, ki_ref, vt_ref, bias_ref, o_ref,
                      keys_ref, madd_ref, qm_ref, qim_ref, lg_ref, *, topk, idx_bits):
    i = pl.program_id(1)
    tq = q_ref.shape[0]
    per_big = KB // KC
    nbig = (i + per_big) // per_big

    lane_hi = lax.broadcasted_iota(I32, (1, LANES), 1) >= HEAD_DIM
    for h in range(N_HEADS):
        hp, sub = divmod(h, 2)
        keep = lane_hi if sub == 1 else jnp.logical_not(lane_hi)
        sl = slice(hp * LANES, (hp + 1) * LANES)
        qm_ref[h] = jnp.where(keep, q_ref[:, sl], jnp.zeros((), BF16))
        qim_ref[h] = jnp.where(keep, qi_ref[:, sl], jnp.zeros((), BF16))

    wt = wt_ref[0]
    qidx = i * tq + lax.broadcasted_iota(I32, (1, tq), 1)

    def score_body(jb, _):
        for u in range(per_big):
            off = pl.multiple_of(jb * KB + u * KC, KC)
            kc = ki_ref[pl.ds(off, KC), :]
            s = jnp.zeros((KC, tq), F32)
            for h in range(N_IDX_HEADS):
                d = _nt_dot(kc, qim_ref[h])
                s = s + wt[h:h + 1, :] * jnp.maximum(d, 0.0)
            kidx = off + lax.broadcasted_iota(I32, (KC, 1), 0)
            keys_ref[pl.ds(off, KC), :] = jnp.where(kidx <= qidx, _sortable_key(s), INT_MIN)
        return 0

    lax.fori_loop(0, nbig, score_body, 0)

    thr, jcut = _select_threshold(keys_ref, nbig, topk, idx_bits)

    def madd_body(jb, _):
        off = pl.multiple_of(jb * KB, KB)
        k = keys_ref[pl.ds(off, KB), :]
        kidx = off + lax.broadcasted_iota(I32, (KB, 1), 0)
        sel = jnp.logical_or(k > thr, jnp.logical_and(k == thr, kidx <= jcut))
        madd_ref[pl.ds(off, KB), :] = jnp.where(sel, 0.0, NEG_INF).astype(F32)
        return 0

    lax.fori_loop(0, nbig, madd_body, 0)

    def groups(x, op):
        return op(x.reshape(x.shape[0] // SUBLANES, SUBLANES, tq), axis=0)

    def head_body(h, _):
        loff = pl.multiple_of((h // 2) * LANES, LANES)
        qh = qm_ref[h]

        def logits_step(jb, _):
            off = pl.multiple_of(jb * KB, KB)
            lg_ref[pl.ds(off, KB), :] = (_nt_dot(k_ref[pl.ds(off, KB), pl.ds(loff, LANES)], qh)
                                         + madd_ref[pl.ds(off, KB), :])
            return 0

        lax.fori_loop(0, nbig, logits_step, 0)

        diag = pl.ds(pl.multiple_of(i * KC, KC), KC)
        lg_ref[diag, :] = lg_ref[diag, :] + bias_ref[0, h]

        @pl.when(i >= 1)
        def _():
            prev = pl.ds(pl.multiple_of((i - 1) * KC, KC), KC)
            lg_ref[prev, :] = lg_ref[prev, :] + bias_ref[1, h]

        def max_step(jb, mx):
            return jnp.maximum(mx, groups(lg_ref[pl.ds(pl.multiple_of(jb * KB, KB), KB), :], jnp.max))

        mx = lax.fori_loop(0, nbig, max_step, jnp.full((SUBLANES, tq), NEG_INF, F32))
        m = jnp.max(mx, axis=0, keepdims=True)
        vrow = pl.multiple_of(h * HEAD_DIM, HEAD_DIM)

        def pv_step(jb, carry):
            ssum, acc = carry
            off = pl.multiple_of(jb * KB, KB)
            p = jnp.exp(lg_ref[pl.ds(off, KB), :] - m)
            vsl = vt_ref[0, pl.ds(vrow, HEAD_DIM), pl.ds(off, KB)]
            return ssum + groups(p, jnp.sum), acc + _dot(vsl, p.astype(BF16))

        ssum, acc = lax.fori_loop(0, nbig, pv_step, (jnp.zeros((SUBLANES, tq), F32),
                                                     jnp.zeros((HEAD_DIM, tq), F32)))
        l = jnp.sum(ssum, axis=0, keepdims=True)
        o_ref[pl.ds(vrow, HEAD_DIM), :] = (acc / l).astype(o_ref.dtype)
        return 0

    lax.fori_loop(0, N_HEADS, head_body, 0)


def _prompt_attention(qb, qib, wit, kb, ki2b, vt, bias_t, batch, seq):
    assert seq % KB == 0 and KB % KC == 0 and TQ == KC
    nq = seq // TQ
    topk = min(TOPK_MAX, seq // 4)
    idx_bits = max(1, (seq - 1).bit_length())
    n = batch * seq
    return pl.pallas_call(
        functools.partial(_prompt_attn_body, topk=topk, idx_bits=idx_bits),
        grid=(batch, nq),
        in_specs=[
            pl.BlockSpec((TQ, D_ATT), lambda b, i: (b * nq + i, 0)),
            pl.BlockSpec((TQ, N_IDX_HEADS * IDX_DIM), lambda b, i: (b * nq + i, 0)),
            pl.BlockSpec((1, N_IDX_HEADS, TQ), lambda b, i: (b, 0, i)),
            pl.BlockSpec((seq, D_ATT), lambda b, i: (b, 0)),
            pl.BlockSpec((seq, 2 * IDX_DIM), lambda b, i: (b, 0)),
            pl.BlockSpec((1, D_ATT, seq), lambda b, i: (b, 0, 0)),
            pl.BlockSpec(bias_t.shape, lambda b, i: (0, 0, 0, 0)),
        ],
        out_specs=pl.BlockSpec((D_ATT, TQ), lambda b, i: (0, b * nq + i)),
        out_shape=jax.ShapeDtypeStruct((D_ATT, n), BF16),
        scratch_shapes=[
            pltpu.VMEM((seq, TQ), I32),
            pltpu.VMEM((seq, TQ), F32),
            pltpu.VMEM((N_HEADS, TQ, LANES), BF16),
            pltpu.VMEM((N_IDX_HEADS, TQ, LANES), BF16),
            pltpu.VMEM((seq, TQ), F32),
        ],
        compiler_params=pltpu.CompilerParams(dimension_semantics=("arbitrary", "arbitrary"),
                                             vmem_limit_bytes=VMEM_LIMIT),
        name="prompt_attention",
    )(qb, qib, wit, kb, ki2b, vt, bias_t)


def _prompt_bias_tiles(rel_bias):
    shifted = (rel_bias - rel_bias[N_BUCKETS - 1:N_BUCKETS]).T.astype(F32)
    c = jnp.arange(KC, dtype=I32)[:, None]
    r = jnp.arange(TQ, dtype=I32)[None, :]
    tiles = []
    for base in (0, TQ):
        dist = base + r - c
        bucket = jnp.where(dist >= 0, _t5_bucket(dist), N_BUCKETS - 1)
        tiles.append(_bucket_lookup(shifted, bucket))
    return jnp.stack(tiles)


def _lru_gates(xc, wa_ref, ba_ref, wx_ref, bx_ref, lam_ref, precision):
    xm = xc.astype(wa_ref.dtype)
    r = jax.nn.sigmoid(_dot(xm, wa_ref[...], precision) + ba_ref[...])
    g = jax.nn.sigmoid(_dot(xm, wx_ref[...], precision) + bx_ref[...])
    log_a = -LRU_C * r * jax.nn.softplus(-lam_ref[...])
    a = jnp.exp(log_a)
    u = jnp.sqrt(-jnp.tanh(log_a) * (a * a + 1.0)) * g * xc
    return a, u


def _rglru_prompt_body(xr_ref, gr_ref, cw_ref, cb_ref, wa_ref, ba_ref, wx_ref, bx_ref, lam_ref,
                       rnn_ref, hlast_ref, ext_ref, hc_ref):
    tt = xr_ref.shape[0]

    @pl.when(pl.program_id(1) == 0)
    def _():
        ext_ref[0:SUBLANES, :] = jnp.zeros((SUBLANES, ext_ref.shape[1]), F32)
        hc_ref[...] = jnp.zeros_like(hc_ref)

    x = xr_ref[...]
    ext_ref[SUBLANES:SUBLANES + tt, :] = x
    cw = cw_ref[...]
    xc = cb_ref[...] + ext_ref[SUBLANES - 3:SUBLANES - 3 + tt, :] * cw[0:1]
    for j in range(1, CONV_W):
        s = CONV_W - 1 - j
        xc = xc + ext_ref[SUBLANES - s:SUBLANES - s + tt, :] * cw[j:j + 1]
    ext_ref[0:SUBLANES, :] = x[tt - SUBLANES:tt]

    a, u = _lru_gates(xc, wa_ref, ba_ref, wx_ref, bx_ref, lam_ref, None)
    row = lax.broadcasted_iota(I32, (tt, 1), 0)
    d = 1
    while d < tt:
        keep = row >= d
        a_sh = jnp.where(keep, pltpu.roll(a, d, 0), 1.0)
        u_sh = jnp.where(keep, pltpu.roll(u, d, 0), 0.0)
        u = u + a * u_sh
        a = a * a_sh
        d *= 2
    h = u + a * hc_ref[...]
    h_last = h[tt - 1:tt]
    hc_ref[...] = h_last
    hlast_ref[0] = h_last
    rnn_ref[...] = (h * jax.nn.gelu(gr_ref[...])).astype(rnn_ref.dtype)


def _rglru_prompt(xr, gr, cw, cb, wa_bd, ba, wx_bd, bx, lam, batch, seq, tt):
    n, dr = xr.shape
    nt = seq // tt
    full = lambda shape: pl.BlockSpec(shape, lambda b, t: (0,) * len(shape))
    return pl.pallas_call(
        _rglru_prompt_body,
        grid=(batch, nt),
        in_specs=[
            pl.BlockSpec((tt, dr), lambda b, t: (b * nt + t, 0)),
            pl.BlockSpec((tt, dr), lambda b, t: (b * nt + t, 0)),
            full(cw.shape), full(cb.shape), full(wa_bd.shape), full(ba.shape),
            full(wx_bd.shape), full(bx.shape), full(lam.shape),
        ],
        out_specs=[
            pl.BlockSpec((tt, dr), lambda b, t: (b * nt + t, 0)),
            pl.BlockSpec((1, 1, dr), lambda b, t: (b, 0, 0)),
        ],
        out_shape=[jax.ShapeDtypeStruct((n, dr), BF16), jax.ShapeDtypeStruct((batch, 1, dr), F32)],
        scratch_shapes=[pltpu.VMEM((tt + SUBLANES, dr), F32), pltpu.VMEM((1, dr), F32)],
        compiler_params=pltpu.CompilerParams(dimension_semantics=("arbitrary", "arbitrary"),
                                             vmem_limit_bytes=VMEM_LIMIT),
        name="rglru_prompt",
    )(xr, gr, cw, cb, wa_bd, ba, wx_bd, bx, lam)


def _rglru_sample_body(xr_ref, gr_ref, cp_ref, hp_ref, cw_ref, cb_ref, wa_ref, ba_ref, wx_ref, bx_ref,
                       lam_ref, rnn_ref, h_ref):
    cw = cw_ref[...]
    xc = cb_ref[...] + cp_ref[0] * cw[0:1]
    for j in range(1, CONV_W - 1):
        xc = xc + cp_ref[j] * cw[j:j + 1]
    xc = xc + xr_ref[...] * cw[CONV_W - 1:CONV_W]
    a, u = _lru_gates(xc, wa_ref, ba_ref, wx_ref, bx_ref, lam_ref, lax.Precision.HIGHEST)
    h = u + a * hp_ref[...]
    h_ref[...] = h
    rnn_ref[...] = h * jax.nn.gelu(gr_ref[...])


def _rglru_sample(xr, gr, conv_prev_t, h_prev, cw, cb, wa_bd, ba, wx_bd, bx, lam):
    n, dr = xr.shape
    return pl.pallas_call(
        _rglru_sample_body,
        out_shape=[jax.ShapeDtypeStruct((n, dr), F32), jax.ShapeDtypeStruct((n, dr), F32)],
        compiler_params=pltpu.CompilerParams(vmem_limit_bytes=VMEM_LIMIT),
        name="rglru_sample",
    )(xr, gr, conv_prev_t, h_prev, cw, cb, wa_bd, ba, wx_bd, bx, lam)


def _layer_norm(x, g, b):
    mu = jnp.mean(x, axis=-1, keepdims=True)
    xc = x - mu
    var = jnp.mean(jnp.square(xc), axis=-1, keepdims=True)
    return xc * lax.rsqrt(var + LN_EPS) * g + b


def _mlp_body(attn_ref, rnn_ref, x_ref, woa_ref, wor_ref, g1_ref, b1_ref, wup_ref, bup_ref,
              wdn_ref, bdn_ref, g2_ref, b2_ref, y_ref, *, alpha, attn_channel_major, ff_chunk):
    cdt = woa_ref.dtype
    if attn_channel_major:
        mix = lax.dot_general(attn_ref[...].astype(cdt), woa_ref[...], (((0,), (0,)), ((), ())),
                              preferred_element_type=F32)
    else:
        mix = _dot(attn_ref[...].astype(cdt), woa_ref[...])
    mix = mix + _dot(rnn_ref[...].astype(cdt), wor_ref[...])
    x1 = _layer_norm(alpha * x_ref[...] + mix, g1_ref[...], b1_ref[...])
    x1c = x1.astype(cdt)
    d_ff = wup_ref.shape[1]
    y = None
    for c in range(0, d_ff, ff_chunk):
        hid = _dot(x1c, wup_ref[:, c:c + ff_chunk]) + bup_ref[:, c:c + ff_chunk]
        hid = jnp.square(jnp.maximum(hid, 0.0)).astype(cdt)
        part = _dot(hid, wdn_ref[c:c + ff_chunk, :])
        y = part if y is None else y + part
    y_ref[...] = _layer_norm(alpha * x1 + y + bdn_ref[...], g2_ref[...], b2_ref[...])


def _out_mlp(attn, rnn, x2d, woa, wor, g1, b1, wup, bup, wdn, bdn, g2, b2, *, alpha, tm,
             attn_channel_major):
    n, dm = x2d.shape
    da = woa.shape[0]
    full = lambda a: pl.BlockSpec(a.shape, lambda i: (0,) * a.ndim)
    attn_spec = (pl.BlockSpec((da, tm), lambda i: (0, i)) if attn_channel_major
                 else pl.BlockSpec((tm, da), lambda i: (i, 0)))
    consts = (woa, wor, g1, b1, wup, bup, wdn, bdn, g2, b2)
    return pl.pallas_call(
        functools.partial(_mlp_body, alpha=alpha, attn_channel_major=attn_channel_major, ff_chunk=1024),
        grid=(n // tm,),
        in_specs=[attn_spec, pl.BlockSpec((tm, rnn.shape[1]), lambda i: (i, 0)),
                  pl.BlockSpec((tm, dm), lambda i: (i, 0))] + [full(a) for a in consts],
        out_specs=pl.BlockSpec((tm, dm), lambda i: (i, 0)),
        out_shape=jax.ShapeDtypeStruct((n, dm), F32),
        compiler_params=pltpu.CompilerParams(dimension_semantics=("arbitrary",),
                                             vmem_limit_bytes=VMEM_LIMIT),
        name="out_mlp",
    )(attn, rnn, x2d, *consts)


def _sample_keys_body(pt_ref, qi_ref, w_ref, kin_ref, cki_ref, keys_ref, knew_ref, kibuf, sem,
                      *, layer):
    b = pl.program_id(0)
    nb = pl.num_programs(0)
    n_pages = kibuf.shape[1]

    def page_copy(bb, slot, p):
        return pltpu.make_async_copy(cki_ref.at[layer, pt_ref[bb, p]], kibuf.at[slot, p], sem.at[slot])

    def issue_all(bb, slot):
        def body(p, _):
            page_copy(bb, slot, p).start()
            return 0
        lax.fori_loop(0, n_pages, body, 0)

    slot = lax.rem(b, 2)

    @pl.when(b == 0)
    def _():
        issue_all(b, slot)

    @pl.when(b + 1 < nb)
    def _():
        issue_all(b + 1, 1 - slot)

    def wait_body(p, _):
        page_copy(b, slot, p).wait()
        return 0

    lax.fori_loop(0, n_pages, wait_body, 0)

    qi16 = qi_ref[0].astype(BF16)
    w = w_ref[0]
    group = 4

    def score_body(g, _):
        for u in range(group):
            p = g * group + u
            d = _dot(qi16, kibuf[slot, p].astype(BF16))
            key = _sortable_key(jnp.sum(w * jnp.maximum(d, 0.0), axis=0, keepdims=True))
            keys_ref[0, pl.ds(p, 1), :] = key
        return 0

    lax.fori_loop(0, n_pages // group, score_body, 0)

    dn = jnp.sum(qi16.astype(F32) * kin_ref[0].astype(BF16).astype(F32), axis=1, keepdims=True)
    key_new = _sortable_key(jnp.sum(w * jnp.maximum(dn, 0.0), axis=0, keepdims=True))
    knew_ref[0] = jnp.broadcast_to(key_new, (1, LANES))


def _sample_keys(page_table, qi_s, wi_col, ki_new, cache_ki_t, layer):
    bd, n_pages = page_table.shape
    assert n_pages % 4 == 0
    grid_spec = pltpu.PrefetchScalarGridSpec(
        num_scalar_prefetch=1,
        grid=(bd,),
        in_specs=[
            pl.BlockSpec((1, N_IDX_HEADS, IDX_DIM), lambda b, pt: (b, 0, 0)),
            pl.BlockSpec((1, N_IDX_HEADS, 1), lambda b, pt: (b, 0, 0)),
            pl.BlockSpec((1, 1, IDX_DIM), lambda b, pt: (b, 0, 0)),
            pl.BlockSpec(memory_space=pl.ANY),
        ],
        out_specs=[
            pl.BlockSpec((1, n_pages, PAGE_SIZE), lambda b, pt: (b, 0, 0)),
            pl.BlockSpec((1, 1, LANES), lambda b, pt: (b, 0, 0)),
        ],
        scratch_shapes=[
            pltpu.VMEM((2, n_pages, IDX_DIM, PAGE_SIZE), F32),
            pltpu.SemaphoreType.DMA((2,)),
        ],
    )
    return pl.pallas_call(
        functools.partial(_sample_keys_body, layer=layer),
        grid_spec=grid_spec,
        out_shape=[jax.ShapeDtypeStruct((bd, n_pages, PAGE_SIZE), I32),
                   jax.ShapeDtypeStruct((bd, 1, LANES), I32)],
        compiler_params=pltpu.CompilerParams(dimension_semantics=("arbitrary",),
                                             vmem_limit_bytes=VMEM_LIMIT),
        name="sample_keys",
    )(page_table, qi_s, wi_col, ki_new, cache_ki_t)


def _sample_mask_body(keys_ref, knew_ref, madd_ref, maddn_ref, *, topk, idx_bits):
    keys = keys_ref[...]
    key_new = knew_ref[:, :, 0:1]
    n_past = keys.shape[1] * keys.shape[2]
    kidx = (lax.broadcasted_iota(I32, keys.shape, 1) * PAGE_SIZE
            + lax.broadcasted_iota(I32, keys.shape, 2))

    def count(pred_past, pred_new):
        c = jnp.sum(jnp.where(pred_past, 1, 0).astype(I32), axis=1, keepdims=True)
        return jnp.sum(c, axis=2, keepdims=True) + jnp.where(pred_new, 1, 0).astype(I32)

    c0 = count(keys >= 0, key_new >= 0)
    nonneg = c0 >= topk
    prefix = jnp.where(nonneg, 0, INT_MIN).astype(I32)
    cge = jnp.where(nonneg, c0, n_past + 1).astype(I32)

    def bit_body(it, carry):
        prefix, cge = carry
        cand = prefix | lax.shift_left(jnp.int32(1), 30 - it)
        c = count(keys >= cand, key_new >= cand)
        ok = c >= topk
        return jnp.where(ok, cand, prefix), jnp.where(ok, c, cge)

    thr, cge = lax.fori_loop(0, 31, bit_body, (prefix, cge))
    real = thr > INT_MIN
    need = jnp.logical_and(cge > topk, real)
    j_default = jnp.where(real, jnp.int32(2 ** 30), jnp.int32(-1))

    def tie_fn():
        want = topk - count(keys > thr, key_new > thr)

        def body(it, p):
            cand = p | lax.shift_left(jnp.int32(1), idx_bits - 1 - it)
            c = count(jnp.logical_and(keys == thr, kidx < cand),
                      jnp.logical_and(key_new == thr, n_past < cand))
            return jnp.where(c < want, cand, p)

        p = lax.fori_loop(0, idx_bits, body, jnp.zeros(thr.shape, I32))
        return jnp.where(need, p, j_default)

    jcut = lax.cond(jnp.max(jnp.where(need, 1, 0)) > 0, tie_fn, lambda: j_default)
    sel = jnp.logical_or(keys > thr, jnp.logical_and(keys == thr, kidx <= jcut))
    sel_new = jnp.logical_or(key_new > thr, jnp.logical_and(key_new == thr, n_past <= jcut))
    madd_ref[...] = jnp.where(sel, 0.0, NEG_INF).astype(F32)
    maddn_ref[...] = jnp.broadcast_to(jnp.where(sel_new, 0.0, NEG_INF).astype(F32), maddn_ref.shape)


def _sample_mask(keys, key_new, topk):
    bd, n_pages, _ = keys.shape
    idx_bits = max(1, (n_pages * PAGE_SIZE).bit_length())
    return pl.pallas_call(
        functools.partial(_sample_mask_body, topk=topk, idx_bits=idx_bits),
        out_shape=[jax.ShapeDtypeStruct(keys.shape, F32), jax.ShapeDtypeStruct(key_new.shape, F32)],
        compiler_params=pltpu.CompilerParams(vmem_limit_bytes=VMEM_LIMIT),
        name="sample_mask",
    )(keys, key_new)


def _sample_attend_body(pt_ref, qb_ref, q_ref, kn_ref, vn_ref, madd_ref, maddn_ref, btab_ref,
                        ck_ref, cv_ref, o_ref, ring_ref, lg_ref, acc_ref, sem, *, layer):
    b = pl.program_id(0)
    n_pages = madd_ref.shape[1]
    depth = ring_ref.shape[0]
    group = 4
    per_seq = 2 * n_pages
    total_items = pl.num_programs(0) * per_seq
    base = b * per_seq

    def item_copy(t, cache_ref, seq_i, page_i):
        slot = lax.rem(t, depth)
        return pltpu.make_async_copy(cache_ref.at[layer, pt_ref[seq_i, page_i]], ring_ref.at[slot],
                                     sem.at[slot])

    def start_item(t):
        seq_i = lax.div(t, per_seq)
        r = t - seq_i * per_seq
        is_v = r >= n_pages

        @pl.when(jnp.logical_not(is_v))
        def _():
            item_copy(t, ck_ref, seq_i, r).start()

        @pl.when(is_v)
        def _():
            item_copy(t, cv_ref, seq_i, r - n_pages).start()

    def finish_item(t):
        item_copy(t, ck_ref, 0, 0).wait()

    def refill(t):
        @pl.when(t + depth < total_items)
        def _():
            start_item(t + depth)

    @pl.when(b == 0)
    def _():
        def prime(t, _):
            start_item(t)
            return 0
        lax.fori_loop(0, depth, prime, 0)

    def k_body(g, _):
        for u in range(group):
            p = g * group + u
            t = base + p
            finish_item(t)
            slot = lax.rem(t, depth)
            mrow = madd_ref[0, pl.ds(p, 1), :]
            is_last = p == n_pages - 1
            for h in range(N_HEADS):
                lg = jnp.sum(ring_ref[slot, h] * qb_ref[0, h], axis=0, keepdims=True) + mrow
                lg_ref[h, pl.ds(p, 1), :] = lg + jnp.where(is_last, btab_ref[h:h + 1, 0:PAGE_SIZE], 0.0)
            refill(t)
        return 0

    lax.fori_loop(0, n_pages // group, k_body, 0)

    def total(x, op):
        return op(op(x, axis=0, keepdims=True), axis=1, keepdims=True)

    lg_new = (jnp.sum(q_ref[0] * kn_ref[0], axis=1, keepdims=True)
              + btab_ref[:, PAGE_SIZE:PAGE_SIZE + 1] + maddn_ref[0][:, 0:1])
    p_new, l = [], []
    for h in range(N_HEADS):
        lgh = lg_ref[h]
        mh = jnp.maximum(total(lgh, jnp.max), lg_new[h:h + 1])
        prh = jnp.exp(lgh - mh)
        lg_ref[h] = prh
        p_new.append(jnp.exp(lg_new[h:h + 1] - mh))
        l.append(total(prh, jnp.sum) + p_new[h])

    acc_ref[...] = jnp.zeros_like(acc_ref)

    def v_body(g, _):
        parts = [None] * N_HEADS
        for u in range(group):
            p = g * group + u
            t = base + n_pages + p
            finish_item(t)
            slot = lax.rem(t, depth)
            for h in range(N_HEADS):
                term = ring_ref[slot, h] * lg_ref[h, pl.ds(p, 1), :]
                parts[h] = term if parts[h] is None else parts[h] + term
            refill(t)
        for h in range(N_HEADS):
            acc_ref[h] += parts[h]
        return 0

    lax.fori_loop(0, n_pages // group, v_body, 0)

    for h in range(N_HEADS):
        past = jnp.sum(acc_ref[h].T, axis=0, keepdims=True)
        o_ref[0, h:h + 1, :] = (past + p_new[h] * vn_ref[0, h:h + 1, :]) / l[h]


def _sample_attend(page_table, q_lanes, q_s, k_new, v_new, madd, madd_new, btab, cache_k_t, cache_v_t,
                   layer):
    bd, n_pages = page_table.shape
    assert n_pages % 4 == 0 and 2 * n_pages >= PAGE_RING
    blk3 = lambda b, pt: (b, 0, 0)
    grid_spec = pltpu.PrefetchScalarGridSpec(
        num_scalar_prefetch=1,
        grid=(bd,),
        in_specs=[
            pl.BlockSpec((1, N_HEADS, HEAD_DIM, PAGE_SIZE), lambda b, pt: (b, 0, 0, 0)),
            pl.BlockSpec((1, N_HEADS, HEAD_DIM), blk3),
            pl.BlockSpec((1, N_HEADS, HEAD_DIM), blk3),
            pl.BlockSpec((1, N_HEADS, HEAD_DIM), blk3),
            pl.BlockSpec((1, n_pages, PAGE_SIZE), blk3),
            pl.BlockSpec((1, 1, LANES), blk3),
            pl.BlockSpec(btab.shape, lambda b, pt: (0, 0)),
            pl.BlockSpec(memory_space=pl.ANY),
            pl.BlockSpec(memory_space=pl.ANY),
        ],
        out_specs=pl.BlockSpec((1, N_HEADS, HEAD_DIM), blk3),
        scratch_shapes=[
            pltpu.VMEM((PAGE_RING, N_HEADS, HEAD_DIM, PAGE_SIZE), F32),
            pltpu.VMEM((N_HEADS, n_pages, PAGE_SIZE), F32),
            pltpu.VMEM((N_HEADS, HEAD_DIM, PAGE_SIZE), F32),
            pltpu.SemaphoreType.DMA((PAGE_RING,)),
        ],
    )
    return pl.pallas_call(
        functools.partial(_sample_attend_body, layer=layer),
        grid_spec=grid_spec,
        out_shape=jax.ShapeDtypeStruct((bd, N_HEADS, HEAD_DIM), F32),
        compiler_params=pltpu.CompilerParams(dimension_semantics=("arbitrary",),
                                             vmem_limit_bytes=VMEM_LIMIT),
        name="sample_attend",
    )(page_table, q_lanes, q_s, k_new, v_new, madd, madd_new, btab, cache_k_t, cache_v_t)


def _block_diag(w):
    nb, c, d = w.shape
    eye = jnp.eye(nb, dtype=w.dtype)
    return (eye[:, None, :, None] * w[:, :, None, :]).reshape(nb * c, nb * d)


def kernel(x_prompt, x_sample, cache_k, cache_v, cache_k_idx, state_h, state_conv, page_table,
           rel_bias, w_in, conv_w, conv_b, w_a, b_a, w_x, b_x, lru_lambda, w_out,
           ln1_g, ln1_b, w_up, b_up, w_down, b_down, ln2_g, ln2_b):
    depth = w_in.shape[0]
    batch, seq, dm = x_prompt.shape
    bd, dec_seq, _ = x_sample.shape
    assert dec_seq == 1, "the decode pass handles one new token per sequence"
    d_rnn = dm - D_ATT
    alpha = (2 * depth) ** 0.25
    n_pages = page_table.shape[1]
    n_past = n_pages * PAGE_SIZE
    topk_s = min(TOPK_MAX, (n_past + dec_seq) // 4)
    row = lambda a: a.reshape(1, -1)
    c16 = lambda a: a.astype(BF16)

    sizes = (D_ATT, D_ATT, D_ATT, N_IDX_HEADS * IDX_DIM, IDX_DIM, N_IDX_HEADS, d_rnn, d_rnn)
    offs = [0]
    for s in sizes:
        offs.append(offs[-1] + s)

    w_in_t = jnp.swapaxes(w_in, 1, 2)
    cache_ki_t = jnp.swapaxes(cache_k_idx, 2, 3)
    cache_k_t = jnp.transpose(cache_k, (0, 1, 3, 4, 2))
    cache_v_t = jnp.transpose(cache_v, (0, 1, 3, 4, 2))

    bias_t = _prompt_bias_tiles(rel_bias)
    shifted = (rel_bias - rel_bias[N_BUCKETS - 1:N_BUCKETS]).T.astype(F32)
    btab = jnp.concatenate([
        _bucket_lookup(shifted, _t5_bucket(PAGE_SIZE - jnp.arange(PAGE_SIZE, dtype=I32))),
        _bucket_lookup(shifted, _t5_bucket(jnp.zeros((PAGE_SIZE,), I32)))], axis=1)

    yp = x_prompt.reshape(batch * seq, dm)
    ys = x_sample.reshape(bd, dm)
    outs = [[] for _ in range(10)]
    for layer in range(depth):
        wt = w_in_t[layer]
        wq, wk, wv, wqi, wki, wwi, wxr, wgr = [wt[offs[j]:offs[j + 1]] for j in range(8)]
        cw, cb, lam = conv_w[layer], row(conv_b[layer]), row(lru_lambda[layer])
        wa_bd, wx_bd = _block_diag(w_a[layer]), _block_diag(w_x[layer])
        ba, bx = row(b_a[layer]), row(b_x[layer])
        woa, wor = c16(w_out[layer][:D_ATT]), c16(w_out[layer][D_ATT:])
        mlp_consts = (woa, wor, row(ln1_g[layer]), row(ln1_b[layer]), c16(w_up[layer]), row(b_up[layer]),
                      c16(w_down[layer]), row(b_down[layer]), row(ln2_g[layer]), row(ln2_b[layer]))

        p_weights = [c16(wq), c16(wk), c16(wk), c16(wv), c16(wqi), c16(jnp.concatenate([wki, wki], 0)),
                     c16(wki), c16(wwi), c16(wxr), c16(wgr)]
        p_plan = (
            (False, ((0, D_ATT, HEAD_DIM ** -0.5, BF16),)),
            (False, ((0, D_ATT, 1.0, BF16),)),
            (True, ((0, D_ATT, 1.0, F32),)),
            (True, ((0, D_ATT, 1.0, F32), (0, D_ATT, 1.0, BF16))),
            (False, ((0, N_IDX_HEADS * IDX_DIM, 1.0, BF16),)),
            (False, ((0, 2 * IDX_DIM, 1.0, BF16),)),
            (True, ((0, IDX_DIM, 1.0, F32),)),
            (True, ((0, N_IDX_HEADS, 1.0, F32),)),
            (False, ((0, d_rnn, 1.0, F32),)),
            (False, ((0, d_rnn, 1.0, F32),)),
        )
        qb, kb, k_t, v_t, vtb, qib, ki2b, ki_t, wit, xr_p, gr_p = _project(
            yp, p_weights, p_plan, batch, seq, tm=512)
        attn_t = _prompt_attention(qb, qib, wit, kb, ki2b, vtb, bias_t, batch, seq)
        rnn_p, h_p = _rglru_prompt(xr_p, gr_p, cw, cb, c16(wa_bd), ba, c16(wx_bd), bx, lam,
                                   batch, seq, tt=512)
        conv_p = xr_p.reshape(batch, seq, d_rnn)[:, seq - (CONV_W - 1):]
        yp = _out_mlp(attn_t, rnn_p, yp, *mlp_consts, alpha=alpha, tm=512, attn_channel_major=True)

        pad_wi = jnp.pad(wwi, ((0, LANES - N_IDX_HEADS), (0, 0)))
        s_weights = [wq, wk, wv, wqi, wki, pad_wi, wxr, wgr]
        s_plan = (
            (False, ((0, D_ATT, HEAD_DIM ** -0.5, F32),)),
            (False, ((0, D_ATT, 1.0, F32),)),
            (False, ((0, D_ATT, 1.0, F32),)),
            (False, ((0, N_IDX_HEADS * IDX_DIM, 1.0, F32),)),
            (False, ((0, IDX_DIM, 1.0, F32),)),
            (False, ((0, LANES, 1.0, F32),)),
            (False, ((0, d_rnn, 1.0, F32),)),
            (False, ((0, d_rnn, 1.0, F32),)),
        )
        q_s, k_s, v_s, qi_s, ki_s, wi_s, xr_s, gr_s = _project(ys, s_weights, s_plan, 1, bd, tm=bd,
                                                               precision=lax.Precision.HIGHEST)
        keys, key_new = _sample_keys(page_table, qi_s.reshape(bd, N_IDX_HEADS, IDX_DIM),
                                     wi_s[:, :N_IDX_HEADS].reshape(bd, N_IDX_HEADS, 1),
                                     ki_s.reshape(bd, 1, IDX_DIM), cache_ki_t, layer)
        madd, madd_new = _sample_mask(keys, key_new, topk_s)
        heads = lambda a: a.reshape(bd, N_HEADS, HEAD_DIM)
        q_lanes = jnp.broadcast_to(heads(q_s)[..., None], (bd, N_HEADS, HEAD_DIM, PAGE_SIZE))
        attn_s = _sample_attend(page_table, q_lanes, heads(q_s), heads(k_s), heads(v_s), madd, madd_new,
                                btab, cache_k_t, cache_v_t, layer)
        conv_prev = state_conv[layer]
        rnn_s, h_s = _rglru_sample(xr_s, gr_s, jnp.swapaxes(conv_prev, 0, 1), state_h[layer], cw, cb,
                                   wa_bd, ba, wx_bd, bx, lam)
        conv_s = jnp.concatenate([conv_prev[:, 1:], xr_s[:, None, :]], axis=1)
        ys = _out_mlp(attn_s.reshape(bd, D_ATT), rnn_s, ys, *mlp_consts, alpha=alpha, tm=bd,
                      attn_channel_major=False)

        per_head_t = lambda a: jnp.transpose(a.reshape(batch, N_HEADS, HEAD_DIM, seq), (0, 3, 1, 2))
        layer_outs = (
            per_head_t(k_t), per_head_t(v_t), jnp.swapaxes(ki_t, 1, 2), h_p.reshape(batch, d_rnn), conv_p,
            k_s.reshape(bd, dec_seq, N_HEADS, HEAD_DIM), v_s.reshape(bd, dec_seq, N_HEADS, HEAD_DIM),
            ki_s.reshape(bd, dec_seq, IDX_DIM), h_s, conv_s,
        )
        for lst, a in zip(outs, layer_outs):
            lst.append(a)

    stacked = [jnp.stack(lst) for lst in outs]
    return (yp.reshape(batch, seq, dm), ys.reshape(bd, dec_seq, dm), *stacked)
```

```python
import functools
import math

import jax
import jax.numpy as jnp
from jax import lax
from jax.experimental import pallas as pl
from jax.experimental.pallas import tpu as pltpu

F32, BF16, I32 = jnp.float32, jnp.bfloat16, jnp.int32

N_HEADS = 8
HEAD_DIM = 64
D_ATT = N_HEADS * HEAD_DIM
N_IDX_HEADS = 8
IDX_DIM = 64
TOPK_MAX = 256
N_RNN_BLOCKS = 8
CONV_W = 4
LRU_C = 8.0
N_BUCKETS = 32
MAX_DISTANCE = 128
LN_EPS = 1e-5
PAGE_SIZE = 128

LANES = 128
SUBLANES = 8
VMEM_LIMIT = 56 * 1024 * 1024

INT_MIN = -(2 ** 31)
NEG_INF = float("-inf")

TQ = 256
KC = 256
KB = 1024
ONES_ROWS = 16
LOG2E = 1.4426950408889634
PAGE_RING = 32


def _nt_dot(a, b, precision=None):
    return lax.dot_general(a, b, (((1,), (1,)), ((), ())), precision=precision,
                           preferred_element_type=F32)


def _dot(a, b, precision=None):
    return jnp.dot(a, b, precision=precision, preferred_element_type=F32)


def _t5_bucket(dist):
    dist = jnp.maximum(dist, 0)
    max_exact = N_BUCKETS // 2
    d = jnp.maximum(dist, 1).astype(F32)
    large = max_exact + (jnp.log(d / max_exact) / math.log(MAX_DISTANCE / max_exact)
                         * (N_BUCKETS - max_exact)).astype(I32)
    large = jnp.minimum(large, N_BUCKETS - 1)
    return jnp.where(dist < max_exact, dist, large)


def _bucket_lookup(table, bucket):
    out = jnp.zeros((table.shape[0],) + bucket.shape, F32)
    for b in range(table.shape[1]):
        out = jnp.where(bucket[None] == b, table[:, b].reshape((-1,) + (1,) * bucket.ndim), out)
    return out


def _sortable_key(s):
    bits = lax.bitcast_convert_type(s, I32)
    return jnp.where(bits >= 0, bits, jnp.int32(INT_MIN) - bits)


def _proj_body(x_ref, *refs, plan, precision):
    n_w = len(plan)
    w_refs, o_refs = refs[:n_w], refs[n_w:]
    x = x_ref[...].astype(w_refs[0].dtype)
    oi = 0
    for (channel_major, outs), w_ref in zip(plan, w_refs):
        y = _nt_dot(w_ref[...], x, precision) if channel_major else _nt_dot(x, w_ref[...], precision)
        for lo, hi, scale, dtype in outs:
            part = y[lo:hi, :] if channel_major else y[:, lo:hi]
            if scale != 1.0:
                part = part * scale
            if channel_major:
                o_refs[oi][0] = part.astype(dtype)
            else:
                o_refs[oi][...] = part.astype(dtype)
            oi += 1


def _project(x2d, weights_t, plan, batch, seq, tm, precision=None):
    n, d = x2d.shape
    assert n == batch * seq and seq % tm == 0
    nt = seq // tm
    in_specs = [pl.BlockSpec((tm, d), lambda b, t: (b * nt + t, 0))]
    for w in weights_t:
        in_specs.append(pl.BlockSpec(w.shape, lambda b, t: (0, 0)))
    out_shapes, out_specs = [], []
    for (channel_major, outs) in plan:
        for lo, hi, _, dtype in outs:
            if channel_major:
                out_shapes.append(jax.ShapeDtypeStruct((batch, hi - lo, seq), dtype))
                out_specs.append(pl.BlockSpec((1, hi - lo, tm), lambda b, t: (b, 0, t)))
            else:
                out_shapes.append(jax.ShapeDtypeStruct((n, hi - lo), dtype))
                out_specs.append(pl.BlockSpec((tm, hi - lo), lambda b, t: (b * nt + t, 0)))
    return pl.pallas_call(
        functools.partial(_proj_body, plan=plan, precision=precision),
        grid=(batch, nt),
        in_specs=in_specs,
        out_specs=out_specs,
        out_shape=out_shapes,
        compiler_params=pltpu.CompilerParams(dimension_semantics=("arbitrary", "arbitrary"),
                                             vmem_limit_bytes=VMEM_LIMIT),
        name="in_proj",
    )(x2d, *weights_t)


FOLD_ROWS = 8 * SUBLANES


def _fold_rows(acc, x, op):
    for r in range(0, x.shape[0], FOLD_ROWS):
        acc = op(acc, x[r:r + FOLD_ROWS])
    return acc


def _count_rows(keys_ref, nbig, pred):
    tq = keys_ref.shape[1]

    def body(j, acc):
        off = pl.multiple_of(j * KB, KB)
        m = pred(keys_ref[pl.ds(off, KB), :], off)
        return _fold_rows(acc, jnp.where(m, 1, 0).astype(I32), jnp.add)

    acc = lax.fori_loop(0, nbig, body, jnp.zeros((FOLD_ROWS, tq), I32))
    return jnp.sum(acc, axis=0, keepdims=True)


def _select_threshold(keys_ref, nch, topk, idx_bits):
    tq = keys_ref.shape[1]
    c0 = _count_rows(keys_ref, nch, lambda k, off: k >= 0)
    nonneg = c0 >= topk
    prefix = jnp.where(nonneg, 0, INT_MIN).astype(I32)
    cge = jnp.where(nonneg, c0, nch * KB).astype(I32)

    def bit_body(it, carry):
        prefix, cge = carry
        cand = prefix | lax.shift_left(jnp.int32(1), 30 - it)
        c = _count_rows(keys_ref, nch, lambda k, off: k >= cand)
        ok = c >= topk
        return jnp.where(ok, cand, prefix), jnp.where(ok, c, cge)

    thr, cge = lax.fori_loop(0, 31, bit_body, (prefix, cge))
    real = thr > INT_MIN
    need = jnp.logical_and(cge > topk, real)
    j_default = jnp.where(real, jnp.int32(2 ** 30), jnp.int32(-1))

    def tie_fn():
        cgt = _count_rows(keys_ref, nch, lambda k, off: k > thr)
        want = topk - cgt

        def body(it, p):
            cand = p | lax.shift_left(jnp.int32(1), idx_bits - 1 - it)

            def pred(k, off):
                kidx = off + lax.broadcasted_iota(I32, (KB, 1), 0)
                return jnp.logical_and(k == thr, kidx < cand)

            c = _count_rows(keys_ref, nch, pred)
            return jnp.where(c < want, cand, p)

        p = lax.fori_loop(0, idx_bits, body, jnp.zeros((1, tq), I32))
        return jnp.where(need, p, j_default)

    any_need = jnp.max(jnp.where(need, 1, 0)) > 0
    jcut = lax.cond(any_need, tie_fn, lambda: j_default)
    return thr, jcut


def _prompt_attn_body(q_ref, qi_ref, wt_ref, k_ref, ki_ref, vt_ref, bias_ref, o_ref,
                      keys_ref, madd_ref, qm_ref, qim_ref, lga_ref, lgb_ref, *, topk, idx_bits):
    i = pl.program_id(1)
    tq = q_ref.shape[0]
    per_big = KB // KC
    nbig = (i + per_big) // per_big

    lane_hi = lax.broadcasted_iota(I32, (1, LANES), 1) >= HEAD_DIM
    for h in range(N_HEADS):
        hp, sub = divmod(h, 2)
        keep = lane_hi if sub == 1 else jnp.logical_not(lane_hi)
        sl = slice(hp * LANES, (hp + 1) * LANES)
        qm_ref[h] = jnp.where(keep, q_ref[:, sl], jnp.zeros((), BF16))
        qim_ref[h] = jnp.where(keep, qi_ref[:, sl], jnp.zeros((), BF16))

    wt = wt_ref[0]
    qidx = i * tq + lax.broadcasted_iota(I32, (1, tq), 1)

    def score_body(jb, _):
        for u in range(per_big):
            off = pl.multiple_of(jb * KB + u * KC, KC)
            kc = ki_ref[pl.ds(off, KC), :]
            s = jnp.zeros((KC, tq), F32)
            for h in range(N_IDX_HEADS):
                d = _nt_dot(kc, qim_ref[h])
                s = s + wt[h:h + 1, :] * jnp.maximum(d, 0.0)
            kidx = off + lax.broadcasted_iota(I32, (KC, 1), 0)
            keys_ref[pl.ds(off, KC), :] = jnp.where(kidx <= qidx, _sortable_key(s), INT_MIN)
        return 0

    lax.fori_loop(0, nbig, score_body, 0)

    thr, jcut = _select_threshold(keys_ref, nbig, topk, idx_bits)

    def madd_body(jb, _):
        off = pl.multiple_of(jb * KB, KB)
        k = keys_ref[pl.ds(off, KB), :]
        kidx = off + lax.broadcasted_iota(I32, (KB, 1), 0)
        sel = jnp.logical_or(k > thr, jnp.logical_and(k == thr, kidx <= jcut))
        madd_ref[pl.ds(off, KB), :] = jnp.where(sel, 0.0, NEG_INF).astype(F32)
        return 0

    lax.fori_loop(0, nbig, madd_body, 0)

    ones_rows = jnp.ones((ONES_ROWS, KB), BF16)
    zero_acc = jnp.zeros((HEAD_DIM + ONES_ROWS, tq), F32)

    def logits_step(h, lg_ref, jb):
        off = pl.multiple_of(jb * KB, KB)
        loff = pl.multiple_of((h // 2) * LANES, LANES)
        lg_ref[pl.ds(off, KB), :] = (_nt_dot(k_ref[pl.ds(off, KB), pl.ds(loff, LANES)], qm_ref[h]) * LOG2E
                                     + madd_ref[pl.ds(off, KB), :])

    def finish_logits(h, lg_ref):
        diag = pl.ds(pl.multiple_of(i * KC, KC), KC)
        lg_ref[diag, :] = lg_ref[diag, :] + bias_ref[0, h]

        @pl.when(i >= 1)
        def _():
            prev = pl.ds(pl.multiple_of((i - 1) * KC, KC), KC)
            lg_ref[prev, :] = lg_ref[prev, :] + bias_ref[1, h]

        def max_step(jb, mx):
            return _fold_rows(mx, lg_ref[pl.ds(pl.multiple_of(jb * KB, KB), KB), :], jnp.maximum)

        mx = lax.fori_loop(0, nbig, max_step, jnp.full((FOLD_ROWS, tq), NEG_INF, F32))
        return jnp.max(mx, axis=0, keepdims=True)

    def pv_step(h, lg_ref, m, jb, acc):
        off = pl.multiple_of(jb * KB, KB)
        p = jnp.exp2(lg_ref[pl.ds(off, KB), :] - m)
        vrow = pl.multiple_of(h * HEAD_DIM, HEAD_DIM)
        vsl = jnp.concatenate([vt_ref[0, pl.ds(vrow, HEAD_DIM), pl.ds(off, KB)], ones_rows], axis=0)
        return acc + _dot(vsl, p.astype(BF16))

    def write_out(h, acc):
        vrow = pl.multiple_of(h * HEAD_DIM, HEAD_DIM)
        o_ref[pl.ds(vrow, HEAD_DIM), :] = (acc[:HEAD_DIM] / acc[HEAD_DIM:HEAD_DIM + 1]).astype(o_ref.dtype)

    def advance(h, lg_new, lg_old, m_old):
        def fused_step(jb, acc):
            logits_step(h, lg_new, jb)
            return pv_step(h - 1, lg_old, m_old, jb, acc)

        write_out(h - 1, lax.fori_loop(0, nbig, fused_step, zero_acc))
        return finish_logits(h, lg_new)

    def first_logits(jb, _):
        logits_step(0, lga_ref, jb)
        return 0

    lax.fori_loop(0, nbig, first_logits, 0)

    def pair_body(t, m_even):
        m_odd = advance(2 * t + 1, lgb_ref, lga_ref, m_even)
        return advance(2 * t + 2, lga_ref, lgb_ref, m_odd)

    m_even = lax.fori_loop(0, N_HEADS // 2 - 1, pair_body, finish_logits(0, lga_ref))
    m_last = advance(N_HEADS - 1, lgb_ref, lga_ref, m_even)
    write_out(N_HEADS - 1, lax.fori_loop(
        0, nbig, lambda jb, acc: pv_step(N_HEADS - 1, lgb_ref, m_last, jb, acc), zero_acc))


def _prompt_attention(qb, qib, wit, kb, ki2b, vt, bias_t, batch, seq):
    assert seq % KB == 0 and KB % KC == 0 and TQ == KC
    nq = seq // TQ
    topk = min(TOPK_MAX, seq // 4)
    idx_bits = max(1, (seq - 1).bit_length())
    n = batch * seq
    return pl.pallas_call(
        functools.partial(_prompt_attn_body, topk=topk, idx_bits=idx_bits),
        grid=(batch, nq),
        in_specs=[
            pl.BlockSpec((TQ, D_ATT), lambda b, i: (b * nq + i, 0)),
            pl.BlockSpec((TQ, N_IDX_HEADS * IDX_DIM), lambda b, i: (b * nq + i, 0)),
            pl.BlockSpec((1, N_IDX_HEADS, TQ), lambda b, i: (b, 0, i)),
            pl.BlockSpec((seq, D_ATT), lambda b, i: (b, 0)),
            pl.BlockSpec((seq, 2 * IDX_DIM), lambda b, i: (b, 0)),
            pl.BlockSpec((1, D_ATT, seq), lambda b, i: (b, 0, 0)),
            pl.BlockSpec(bias_t.shape, lambda b, i: (0, 0, 0, 0)),
        ],
        out_specs=pl.BlockSpec((D_ATT, TQ), lambda b, i: (0, b * nq + i)),
        out_shape=jax.ShapeDtypeStruct((D_ATT, n), BF16),
        scratch_shapes=[
            pltpu.VMEM((seq, TQ), I32),
            pltpu.VMEM((seq, TQ), F32),
            pltpu.VMEM((N_HEADS, TQ, LANES), BF16),
            pltpu.VMEM((N_IDX_HEADS, TQ, LANES), BF16),
            pltpu.VMEM((seq, TQ), F32),
            pltpu.VMEM((seq, TQ), F32),
        ],
        compiler_params=pltpu.CompilerParams(dimension_semantics=("arbitrary", "arbitrary"),
                                             vmem_limit_bytes=VMEM_LIMIT),
        name="prompt_attention",
    )(qb, qib, wit, kb, ki2b, vt, bias_t)


def _prompt_bias_tiles(rel_bias):
    shifted = (rel_bias - rel_bias[N_BUCKETS - 1:N_BUCKETS]).T.astype(F32)
    c = jnp.arange(KC, dtype=I32)[:, None]
    r = jnp.arange(TQ, dtype=I32)[None, :]
    tiles = []
    for base in (0, TQ):
        dist = base + r - c
        bucket = jnp.where(dist >= 0, _t5_bucket(dist), N_BUCKETS - 1)
        tiles.append(_bucket_lookup(shifted, bucket))
    return jnp.stack(tiles) * LOG2E


def _lru_gates(xc, wa_ref, ba_ref, wx_ref, bx_ref, lam_ref, precision):
    xm = xc.astype(wa_ref.dtype)
    r = jax.nn.sigmoid(_dot(xm, wa_ref[...], precision) + ba_ref[...])
    g = jax.nn.sigmoid(_dot(xm, wx_ref[...], precision) + bx_ref[...])
    log_a = -LRU_C * r * jax.nn.softplus(-lam_ref[...])
    a = jnp.exp(log_a)
    u = jnp.sqrt(-jnp.tanh(log_a) * (a * a + 1.0)) * g * xc
    return a, u


def _rglru_prompt_body(xr_ref, gr_ref, cw_ref, cb_ref, wa_ref, ba_ref, wx_ref, bx_ref, lam_ref,
                       rnn_ref, hlast_ref, ext_ref, hc_ref):
    tt = xr_ref.shape[0]

    @pl.when(pl.program_id(1) == 0)
    def _():
        ext_ref[0:SUBLANES, :] = jnp.zeros((SUBLANES, ext_ref.shape[1]), F32)
        hc_ref[...] = jnp.zeros_like(hc_ref)

    x = xr_ref[...]
    ext_ref[SUBLANES:SUBLANES + tt, :] = x
    cw = cw_ref[...]
    xc = cb_ref[...] + ext_ref[SUBLANES - 3:SUBLANES - 3 + tt, :] * cw[0:1]
    for j in range(1, CONV_W):
        s = CONV_W - 1 - j
        xc = xc + ext_ref[SUBLANES - s:SUBLANES - s + tt, :] * cw[j:j + 1]
    ext_ref[0:SUBLANES, :] = x[tt - SUBLANES:tt]

    a, u = _lru_gates(xc, wa_ref, ba_ref, wx_ref, bx_ref, lam_ref, None)
    row = lax.broadcasted_iota(I32, (tt, 1), 0)
    d = 1
    while d < tt:
        keep = row >= d
        a_sh = jnp.where(keep, pltpu.roll(a, d, 0), 1.0)
        u_sh = jnp.where(keep, pltpu.roll(u, d, 0), 0.0)
        u = u + a * u_sh
        a = a * a_sh
        d *= 2
    h = u + a * hc_ref[...]
    h_last = h[tt - 1:tt]
    hc_ref[...] = h_last
    hlast_ref[0] = h_last
    rnn_ref[...] = (h * jax.nn.gelu(gr_ref[...])).astype(rnn_ref.dtype)


def _rglru_prompt(xr, gr, cw, cb, wa_bd, ba, wx_bd, bx, lam, batch, seq, tt):
    n, dr = xr.shape
    nt = seq // tt
    full = lambda shape: pl.BlockSpec(shape, lambda b, t: (0,) * len(shape))
    return pl.pallas_call(
        _rglru_prompt_body,
        grid=(batch, nt),
        in_specs=[
            pl.BlockSpec((tt, dr), lambda b, t: (b * nt + t, 0)),
            pl.BlockSpec((tt, dr), lambda b, t: (b * nt + t, 0)),
            full(cw.shape), full(cb.shape), full(wa_bd.shape), full(ba.shape),
            full(wx_bd.shape), full(bx.shape), full(lam.shape),
        ],
        out_specs=[
            pl.BlockSpec((tt, dr), lambda b, t: (b * nt + t, 0)),
            pl.BlockSpec((1, 1, dr), lambda b, t: (b, 0, 0)),
        ],
        out_shape=[jax.ShapeDtypeStruct((n, dr), BF16), jax.ShapeDtypeStruct((batch, 1, dr), F32)],
        scratch_shapes=[pltpu.VMEM((tt + SUBLANES, dr), F32), pltpu.VMEM((1, dr), F32)],
        compiler_params=pltpu.CompilerParams(dimension_semantics=("arbitrary", "arbitrary"),
                                             vmem_limit_bytes=VMEM_LIMIT),
        name="rglru_prompt",
    )(xr, gr, cw, cb, wa_bd, ba, wx_bd, bx, lam)


def _rglru_sample_body(xr_ref, gr_ref, cp_ref, hp_ref, cw_ref, cb_ref, wa_ref, ba_ref, wx_ref, bx_ref,
                       lam_ref, rnn_ref, h_ref):
    cw = cw_ref[...]
    xc = cb_ref[...] + cp_ref[0] * cw[0:1]
    for j in range(1, CONV_W - 1):
        xc = xc + cp_ref[j] * cw[j:j + 1]
    xc = xc + xr_ref[...] * cw[CONV_W - 1:CONV_W]
    a, u = _lru_gates(xc, wa_ref, ba_ref, wx_ref, bx_ref, lam_ref, lax.Precision.HIGHEST)
    h = u + a * hp_ref[...]
    h_ref[...] = h
    rnn_ref[...] = h * jax.nn.gelu(gr_ref[...])


def _rglru_sample(xr, gr, conv_prev_t, h_prev, cw, cb, wa_bd, ba, wx_bd, bx, lam):
    n, dr = xr.shape
    return pl.pallas_call(
        _rglru_sample_body,
        out_shape=[jax.ShapeDtypeStruct((n, dr), F32), jax.ShapeDtypeStruct((n, dr), F32)],
        compiler_params=pltpu.CompilerParams(vmem_limit_bytes=VMEM_LIMIT),
        name="rglru_sample",
    )(xr, gr, conv_prev_t, h_prev, cw, cb, wa_bd, ba, wx_bd, bx, lam)


def _layer_norm(x, g, b):
    mu = jnp.mean(x, axis=-1, keepdims=True)
    xc = x - mu
    var = jnp.mean(jnp.square(xc), axis=-1, keepdims=True)
    return xc * lax.rsqrt(var + LN_EPS) * g + b


def _mlp_body(attn_ref, rnn_ref, x_ref, woa_ref, wor_ref, g1_ref, b1_ref, wup_ref, bup_ref,
              wdn_ref, bdn_ref, g2_ref, b2_ref, y_ref, *, alpha, attn_channel_major, ff_chunk):
    cdt = woa_ref.dtype
    if attn_channel_major:
        mix = lax.dot_general(attn_ref[...].astype(cdt), woa_ref[...], (((0,), (0,)), ((), ())),
                              preferred_element_type=F32)
    else:
        mix = _dot(attn_ref[...].astype(cdt), woa_ref[...])
    mix = mix + _dot(rnn_ref[...].astype(cdt), wor_ref[...])
    x1 = _layer_norm(alpha * x_ref[...] + mix, g1_ref[...], b1_ref[...])
    x1c = x1.astype(cdt)
    d_ff = wup_ref.shape[1]
    y = None
    for c in range(0, d_ff, ff_chunk):
        hid = _dot(x1c, wup_ref[:, c:c + ff_chunk]) + bup_ref[:, c:c + ff_chunk]
        hid = jnp.square(jnp.maximum(hid, 0.0)).astype(cdt)
        part = _dot(hid, wdn_ref[c:c + ff_chunk, :])
        y = part if y is None else y + part
    y_ref[...] = _layer_norm(alpha * x1 + y + bdn_ref[...], g2_ref[...], b2_ref[...])


def _out_mlp(attn, rnn, x2d, woa, wor, g1, b1, wup, bup, wdn, bdn, g2, b2, *, alpha, tm,
             attn_channel_major):
    n, dm = x2d.shape
    da = woa.shape[0]
    full = lambda a: pl.BlockSpec(a.shape, lambda i: (0,) * a.ndim)
    attn_spec = (pl.BlockSpec((da, tm), lambda i: (0, i)) if attn_channel_major
                 else pl.BlockSpec((tm, da), lambda i: (i, 0)))
    consts = (woa, wor, g1, b1, wup, bup, wdn, bdn, g2, b2)
    return pl.pallas_call(
        functools.partial(_mlp_body, alpha=alpha, attn_channel_major=attn_channel_major, ff_chunk=1024),
        grid=(n // tm,),
        in_specs=[attn_spec, pl.BlockSpec((tm, rnn.shape[1]), lambda i: (i, 0)),
                  pl.BlockSpec((tm, dm), lambda i: (i, 0))] + [full(a) for a in consts],
        out_specs=pl.BlockSpec((tm, dm), lambda i: (i, 0)),
        out_shape=jax.ShapeDtypeStruct((n, dm), F32),
        compiler_params=pltpu.CompilerParams(dimension_semantics=("arbitrary",),
                                             vmem_limit_bytes=VMEM_LIMIT),
        name="out_mlp",
    )(attn, rnn, x2d, *consts)


def _sample_keys_body(pt_ref, qi_ref, w_ref, kin_ref, cki_ref, keys_ref, knew_ref, kibuf, sem,
                      *, layer):
    b = pl.program_id(0)
    nb = pl.num_programs(0)
    n_pages = kibuf.shape[1]

    def page_copy(bb, slot, p):
        return pltpu.make_async_copy(cki_ref.at[layer, pt_ref[bb, p]], kibuf.at[slot, p], sem.at[slot])

    def issue_all(bb, slot):
        def body(p, _):
            page_copy(bb, slot, p).start()
            return 0
        lax.fori_loop(0, n_pages, body, 0)

    slot = lax.rem(b, 2)

    @pl.when(b == 0)
    def _():
        issue_all(b, slot)

    @pl.when(b + 1 < nb)
    def _():
        issue_all(b + 1, 1 - slot)

    def wait_body(p, _):
        page_copy(b, slot, p).wait()
        return 0

    lax.fori_loop(0, n_pages, wait_body, 0)

    qi16 = qi_ref[0].astype(BF16)
    w = w_ref[0]
    group = 4

    def score_body(g, _):
        for u in range(group):
            p = g * group + u
            d = _dot(qi16, kibuf[slot, p].astype(BF16))
            key = _sortable_key(jnp.sum(w * jnp.maximum(d, 0.0), axis=0, keepdims=True))
            keys_ref[0, pl.ds(p, 1), :] = key
        return 0

    lax.fori_loop(0, n_pages // group, score_body, 0)

    dn = jnp.sum(qi16.astype(F32) * kin_ref[0].astype(BF16).astype(F32), axis=1, keepdims=True)
    key_new = _sortable_key(jnp.sum(w * jnp.maximum(dn, 0.0), axis=0, keepdims=True))
    knew_ref[0] = jnp.broadcast_to(key_new, (1, LANES))


def _sample_keys(page_table, qi_s, wi_col, ki_new, cache_ki_t, layer):
    bd, n_pages = page_table.shape
    assert n_pages % 4 == 0
    grid_spec = pltpu.PrefetchScalarGridSpec(
        num_scalar_prefetch=1,
        grid=(bd,),
        in_specs=[
            pl.BlockSpec((1, N_IDX_HEADS, IDX_DIM), lambda b, pt: (b, 0, 0)),
            pl.BlockSpec((1, N_IDX_HEADS, 1), lambda b, pt: (b, 0, 0)),
            pl.BlockSpec((1, 1, IDX_DIM), lambda b, pt: (b, 0, 0)),
            pl.BlockSpec(memory_space=pl.ANY),
        ],
        out_specs=[
            pl.BlockSpec((1, n_pages, PAGE_SIZE), lambda b, pt: (b, 0, 0)),
            pl.BlockSpec((1, 1, LANES), lambda b, pt: (b, 0, 0)),
        ],
        scratch_shapes=[
            pltpu.VMEM((2, n_pages, IDX_DIM, PAGE_SIZE), F32),
            pltpu.SemaphoreType.DMA((2,)),
        ],
    )
    return pl.pallas_call(
        functools.partial(_sample_keys_body, layer=layer),
        grid_spec=grid_spec,
        out_shape=[jax.ShapeDtypeStruct((bd, n_pages, PAGE_SIZE), I32),
                   jax.ShapeDtypeStruct((bd, 1, LANES), I32)],
        compiler_params=pltpu.CompilerParams(dimension_semantics=("arbitrary",),
                                             vmem_limit_bytes=VMEM_LIMIT),
        name="sample_keys",
    )(page_table, qi_s, wi_col, ki_new, cache_ki_t)


def _sample_mask_body(keys_ref, knew_ref, madd_ref, maddn_ref, *, topk, idx_bits):
    keys = keys_ref[...]
    key_new = knew_ref[:, :, 0:1]
    n_past = keys.shape[1] * keys.shape[2]
    kidx = (lax.broadcasted_iota(I32, keys.shape, 1) * PAGE_SIZE
            + lax.broadcasted_iota(I32, keys.shape, 2))

    def count(pred_past, pred_new):
        c = jnp.sum(jnp.where(pred_past, 1, 0).astype(I32), axis=1, keepdims=True)
        return jnp.sum(c, axis=2, keepdims=True) + jnp.where(pred_new, 1, 0).astype(I32)

    c0 = count(keys >= 0, key_new >= 0)
    nonneg = c0 >= topk
    prefix = jnp.where(nonneg, 0, INT_MIN).astype(I32)
    cge = jnp.where(nonneg, c0, n_past + 1).astype(I32)

    def bit_body(it, carry):
        prefix, cge = carry
        cand = prefix | lax.shift_left(jnp.int32(1), 30 - it)
        c = count(keys >= cand, key_new >= cand)
        ok = c >= topk
        return jnp.where(ok, cand, prefix), jnp.where(ok, c, cge)

    thr, cge = lax.fori_loop(0, 31, bit_body, (prefix, cge))
    real = thr > INT_MIN
    need = jnp.logical_and(cge > topk, real)
    j_default = jnp.where(real, jnp.int32(2 ** 30), jnp.int32(-1))

    def tie_fn():
        want = topk - count(keys > thr, key_new > thr)

        def body(it, p):
            cand = p | lax.shift_left(jnp.int32(1), idx_bits - 1 - it)
            c = count(jnp.logical_and(keys == thr, kidx < cand),
                      jnp.logical_and(key_new == thr, n_past < cand))
            return jnp.where(c < want, cand, p)

        p = lax.fori_loop(0, idx_bits, body, jnp.zeros(thr.shape, I32))
        return jnp.where(need, p, j_default)

    jcut = lax.cond(jnp.max(jnp.where(need, 1, 0)) > 0, tie_fn, lambda: j_default)
    sel = jnp.logical_or(keys > thr, jnp.logical_and(keys == thr, kidx <= jcut))
    sel_new = jnp.logical_or(key_new > thr, jnp.logical_and(key_new == thr, n_past <= jcut))
    madd_ref[...] = jnp.where(sel, 0.0, NEG_INF).astype(F32)
    maddn_ref[...] = jnp.broadcast_to(jnp.where(sel_new, 0.0, NEG_INF).astype(F32), maddn_ref.shape)


def _sample_mask(keys, key_new, topk):
    bd, n_pages, _ = keys.shape
    idx_bits = max(1, (n_pages * PAGE_SIZE).bit_length())
    return pl.pallas_call(
        functools.partial(_sample_mask_body, topk=topk, idx_bits=idx_bits),
        out_shape=[jax.ShapeDtypeStruct(keys.shape, F32), jax.ShapeDtypeStruct(key_new.shape, F32)],
        compiler_params=pltpu.CompilerParams(vmem_limit_bytes=VMEM_LIMIT),
        name="sample_mask",
    )(keys, key_new)


def _sample_attend_body(pt_ref, qb_ref, q_ref, kn_ref, vn_ref, madd_ref, maddn_ref, btab_ref,
                        ck_ref, cv_ref, o_ref, ring_ref, lg_ref, acc_ref, sem, *, layer):
    b = pl.program_id(0)
    n_pages = madd_ref.shape[1]
    depth = ring_ref.shape[0]
    group = 4
    per_seq = 2 * n_pages
    total_items = pl.num_programs(0) * per_seq
    base = b * per_seq

    def item_copy(t, cache_ref, seq_i, page_i):
        slot = lax.rem(t, depth)
        return pltpu.make_async_copy(cache_ref.at[layer, pt_ref[seq_i, page_i]], ring_ref.at[slot],
                                     sem.at[slot])

    def start_item(t):
        seq_i = lax.div(t, per_seq)
        r = t - seq_i * per_seq
        is_v = r >= n_pages

        @pl.when(jnp.logical_not(is_v))
        def _():
            item_copy(t, ck_ref, seq_i, r).start()

        @pl.when(is_v)
        def _():
            item_copy(t, cv_ref, seq_i, r - n_pages).start()

    def finish_item(t):
        item_copy(t, ck_ref, 0, 0).wait()

    def refill(t):
        @pl.when(t + depth < total_items)
        def _():
            start_item(t + depth)

    @pl.when(b == 0)
    def _():
        def prime(t, _):
            start_item(t)
            return 0
        lax.fori_loop(0, depth, prime, 0)

    def k_body(g, _):
        for u in range(group):
            p = g * group + u
            t = base + p
            finish_item(t)
            slot = lax.rem(t, depth)
            mrow = madd_ref[0, pl.ds(p, 1), :]
            is_last = p == n_pages - 1
            for h in range(N_HEADS):
                lg = jnp.sum(ring_ref[slot, h] * qb_ref[0, h], axis=0, keepdims=True) + mrow
                lg_ref[h, pl.ds(p, 1), :] = lg + jnp.where(is_last, btab_ref[h:h + 1, 0:PAGE_SIZE], 0.0)
            refill(t)
        return 0

    lax.fori_loop(0, n_pages // group, k_body, 0)

    def total(x, op):
        return op(op(x, axis=0, keepdims=True), axis=1, keepdims=True)

    lg_new = (jnp.sum(q_ref[0] * kn_ref[0], axis=1, keepdims=True)
              + btab_ref[:, PAGE_SIZE:PAGE_SIZE + 1] + maddn_ref[0][:, 0:1])
    p_new, l = [], []
    for h in range(N_HEADS):
        lgh = lg_ref[h]
        mh = jnp.maximum(total(lgh, jnp.max), lg_new[h:h + 1])
        prh = jnp.exp(lgh - mh)
        lg_ref[h] = prh
        p_new.append(jnp.exp(lg_new[h:h + 1] - mh))
        l.append(total(prh, jnp.sum) + p_new[h])

    acc_ref[...] = jnp.zeros_like(acc_ref)

    def v_body(g, _):
        parts = [None] * N_HEADS
        for u in range(group):
            p = g * group + u
            t = base + n_pages + p
            finish_item(t)
            slot = lax.rem(t, depth)
            for h in range(N_HEADS):
                term = ring_ref[slot, h] * lg_ref[h, pl.ds(p, 1), :]
                parts[h] = term if parts[h] is None else parts[h] + term
            refill(t)
        for h in range(N_HEADS):
            acc_ref[h] += parts[h]
        return 0

    lax.fori_loop(0, n_pages // group, v_body, 0)

    for h in range(N_HEADS):
        past = jnp.sum(acc_ref[h].T, axis=0, keepdims=True)
        o_ref[0, h:h + 1, :] = (past + p_new[h] * vn_ref[0, h:h + 1, :]) / l[h]


def _sample_attend(page_table, q_lanes, q_s, k_new, v_new, madd, madd_new, btab, cache_k_t, cache_v_t,
                   layer):
    bd, n_pages = page_table.shape
    assert n_pages % 4 == 0 and 2 * n_pages >= PAGE_RING
    blk3 = lambda b, pt: (b, 0, 0)
    grid_spec = pltpu.PrefetchScalarGridSpec(
        num_scalar_prefetch=1,
        grid=(bd,),
        in_specs=[
            pl.BlockSpec((1, N_HEADS, HEAD_DIM, PAGE_SIZE), lambda b, pt: (b, 0, 0, 0)),
            pl.BlockSpec((1, N_HEADS, HEAD_DIM), blk3),
            pl.BlockSpec((1, N_HEADS, HEAD_DIM), blk3),
            pl.BlockSpec((1, N_HEADS, HEAD_DIM), blk3),
            pl.BlockSpec((1, n_pages, PAGE_SIZE), blk3),
            pl.BlockSpec((1, 1, LANES), blk3),
            pl.BlockSpec(btab.shape, lambda b, pt: (0, 0)),
            pl.BlockSpec(memory_space=pl.ANY),
            pl.BlockSpec(memory_space=pl.ANY),
        ],
        out_specs=pl.BlockSpec((1, N_HEADS, HEAD_DIM), blk3),
        scratch_shapes=[
            pltpu.VMEM((PAGE_RING, N_HEADS, HEAD_DIM, PAGE_SIZE), F32),
            pltpu.VMEM((N_HEADS, n_pages, PAGE_SIZE), F32),
            pltpu.VMEM((N_HEADS, HEAD_DIM, PAGE_SIZE), F32),
            pltpu.SemaphoreType.DMA((PAGE_RING,)),
        ],
    )
    return pl.pallas_call(
        functools.partial(_sample_attend_body, layer=layer),
        grid_spec=grid_spec,
        out_shape=jax.ShapeDtypeStruct((bd, N_HEADS, HEAD_DIM), F32),
        compiler_params=pltpu.CompilerParams(dimension_semantics=("arbitrary",),
                                             vmem_limit_bytes=VMEM_LIMIT),
        name="sample_attend",
    )(page_table, q_lanes, q_s, k_new, v_new, madd, madd_new, btab, cache_k_t, cache_v_t)


def _block_diag(w):
    nb, c, d = w.shape
    eye = jnp.eye(nb, dtype=w.dtype)
    return (eye[:, None, :, None] * w[:, :, None, :]).reshape(nb * c, nb * d)


def kernel(x_prompt, x_sample, cache_k, cache_v, cache_k_idx, state_h, state_conv, page_table,
           rel_bias, w_in, conv_w, conv_b, w_a, b_a, w_x, b_x, lru_lambda, w_out,
           ln1_g, ln1_b, w_up, b_up, w_down, b_down, ln2_g, ln2_b):
    depth = w_in.shape[0]
    batch, seq, dm = x_prompt.shape
    bd, dec_seq, _ = x_sample.shape
    assert dec_seq == 1, "the decode pass handles one new token per sequence"
    d_rnn = dm - D_ATT
    alpha = (2 * depth) ** 0.25
    n_pages = page_table.shape[1]
    n_past = n_pages * PAGE_SIZE
    topk_s = min(TOPK_MAX, (n_past + dec_seq) // 4)
    row = lambda a: a.reshape(1, -1)
    c16 = lambda a: a.astype(BF16)

    sizes = (D_ATT, D_ATT, D_ATT, N_IDX_HEADS * IDX_DIM, IDX_DIM, N_IDX_HEADS, d_rnn, d_rnn)
    offs = [0]
    for s in sizes:
        offs.append(offs[-1] + s)

    w_in_t = jnp.swapaxes(w_in, 1, 2)
    cache_ki_t = jnp.swapaxes(cache_k_idx, 2, 3)
    cache_k_t = jnp.transpose(cache_k, (0, 1, 3, 4, 2))
    cache_v_t = jnp.transpose(cache_v, (0, 1, 3, 4, 2))

    bias_t = _prompt_bias_tiles(rel_bias)
    shifted = (rel_bias - rel_bias[N_BUCKETS - 1:N_BUCKETS]).T.astype(F32)
    btab = jnp.concatenate([
        _bucket_lookup(shifted, _t5_bucket(PAGE_SIZE - jnp.arange(PAGE_SIZE, dtype=I32))),
        _bucket_lookup(shifted, _t5_bucket(jnp.zeros((PAGE_SIZE,), I32)))], axis=1)

    yp = x_prompt.reshape(batch * seq, dm)
    ys = x_sample.reshape(bd, dm)
    outs = [[] for _ in range(10)]
    for layer in range(depth):
        wt = w_in_t[layer]
        wq, wk, wv, wqi, wki, wwi, wxr, wgr = [wt[offs[j]:offs[j + 1]] for j in range(8)]
        cw, cb, lam = conv_w[layer], row(conv_b[layer]), row(lru_lambda[layer])
        wa_bd, wx_bd = _block_diag(w_a[layer]), _block_diag(w_x[layer])
        ba, bx = row(b_a[layer]), row(b_x[layer])
        woa, wor = c16(w_out[layer][:D_ATT]), c16(w_out[layer][D_ATT:])
        mlp_consts = (woa, wor, row(ln1_g[layer]), row(ln1_b[layer]), c16(w_up[layer]), row(b_up[layer]),
                      c16(w_down[layer]), row(b_down[layer]), row(ln2_g[layer]), row(ln2_b[layer]))

        p_weights = [c16(wq), c16(wk), c16(wk), c16(wv), c16(wqi), c16(jnp.concatenate([wki, wki], 0)),
                     c16(wki), c16(wwi), c16(wxr), c16(wgr)]
        p_plan = (
            (False, ((0, D_ATT, HEAD_DIM ** -0.5, BF16),)),
            (False, ((0, D_ATT, 1.0, BF16),)),
            (True, ((0, D_ATT, 1.0, F32),)),
            (True, ((0, D_ATT, 1.0, F32), (0, D_ATT, 1.0, BF16))),
            (False, ((0, N_IDX_HEADS * IDX_DIM, 1.0, BF16),)),
            (False, ((0, 2 * IDX_DIM, 1.0, BF16),)),
            (True, ((0, IDX_DIM, 1.0, F32),)),
            (True, ((0, N_IDX_HEADS, 1.0, F32),)),
            (False, ((0, d_rnn, 1.0, F32),)),
            (False, ((0, d_rnn, 1.0, F32),)),
        )
        qb, kb, k_t, v_t, vtb, qib, ki2b, ki_t, wit, xr_p, gr_p = _project(
            yp, p_weights, p_plan, batch, seq, tm=512)
        attn_t = _prompt_attention(qb, qib, wit, kb, ki2b, vtb, bias_t, batch, seq)
        rnn_p, h_p = _rglru_prompt(xr_p, gr_p, cw, cb, c16(wa_bd), ba, c16(wx_bd), bx, lam,
                                   batch, seq, tt=512)
        conv_p = xr_p.reshape(batch, seq, d_rnn)[:, seq - (CONV_W - 1):]
        yp = _out_mlp(attn_t, rnn_p, yp, *mlp_consts, alpha=alpha, tm=512, attn_channel_major=True)

        pad_wi = jnp.pad(wwi, ((0, LANES - N_IDX_HEADS), (0, 0)))
        s_weights = [wq, wk, wv, wqi, wki, pad_wi, wxr, wgr]
        s_plan = (
            (False, ((0, D_ATT, HEAD_DIM ** -0.5, F32),)),
            (False, ((0, D_ATT, 1.0, F32),)),
            (False, ((0, D_ATT, 1.0, F32),)),
            (False, ((0, N_IDX_HEADS * IDX_DIM, 1.0, F32),)),
            (False, ((0, IDX_DIM, 1.0, F32),)),
            (False, ((0, LANES, 1.0, F32),)),
            (False, ((0, d_rnn, 1.0, F32),)),
            (False, ((0, d_rnn, 1.0, F32),)),
        )
        q_s, k_s, v_s, qi_s, ki_s, wi_s, xr_s, gr_s = _project(ys, s_weights, s_plan, 1, bd, tm=bd,
                                                               precision=lax.Precision.HIGHEST)
        keys, key_new = _sample_keys(page_table, qi_s.reshape(bd, N_IDX_HEADS, IDX_DIM),
                                     wi_s[:, :N_IDX_HEADS].reshape(bd, N_IDX_HEADS, 1),
                                     ki_s.reshape(bd, 1, IDX_DIM), cache_ki_t, layer)
        madd, madd_new = _sample_mask(keys, key_new, topk_s)
        heads = lambda a: a.reshape(bd, N_HEADS, HEAD_DIM)
        q_lanes = jnp.broadcast_to(heads(q_s)[..., None], (bd, N_HEADS, HEAD_DIM, PAGE_SIZE))
        attn_s = _sample_attend(page_table, q_lanes, heads(q_s), heads(k_s), heads(v_s), madd, madd_new,
                                btab, cache_k_t, cache_v_t, layer)
        conv_prev = state_conv[layer]
        rnn_s, h_s = _rglru_sample(xr_s, gr_s, jnp.swapaxes(conv_prev, 0, 1), state_h[layer], cw, cb,
                                   wa_bd, ba, wx_bd, bx, lam)
        conv_s = jnp.concatenate([conv_prev[:, 1:], xr_s[:, None, :]], axis=1)
        ys = _out_mlp(attn_s.reshape(bd, D_ATT), rnn_s, ys, *mlp_consts, alpha=alpha, tm=bd,
                      attn_channel_major=False)

        per_head_t = lambda a: jnp.transpose(a.reshape(batch, N_HEADS, HEAD_DIM, seq), (0, 3, 1, 2))
        layer_outs = (
            per_head_t(k_t), per_head_t(v_t), jnp.swapaxes(ki_t, 1, 2), h_p.reshape(batch, d_rnn), conv_p,
            k_s.reshape(bd, dec_seq, N_HEADS, HEAD_DIM), v_s.reshape(bd, dec_seq, N_HEADS, HEAD_DIM),
            ki_s.reshape(bd, dec_seq, IDX_DIM), h_s, conv_s,
        )
        for lst, a in zip(outs, layer_outs):
            lst.append(a)

    stacked = [jnp.stack(lst) for lst in outs]
    return (yp.reshape(batch, seq, dm), ys.reshape(bd, dec_seq, dm), *stacked)
```

```python
import functools
import math

import jax
import jax.numpy as jnp
from jax import lax
from jax.experimental import pallas as pl
from jax.experimental.pallas import tpu as pltpu

F32, BF16, I32 = jnp.float32, jnp.bfloat16, jnp.int32

N_HEADS = 8
HEAD_DIM = 64
D_ATT = N_HEADS * HEAD_DIM
N_IDX_HEADS = 8
IDX_DIM = 64
TOPK_MAX = 256
N_RNN_BLOCKS = 8
CONV_W = 4
LRU_C = 8.0
N_BUCKETS = 32
MAX_DISTANCE = 128
LN_EPS = 1e-5
PAGE_SIZE = 128

LANES = 128
SUBLANES = 8
VMEM_LIMIT = 56 * 1024 * 1024

INT_MIN = -(2 ** 31)
NEG_INF = float("-inf")

TQ = 256
KC = 256
KB = 1024
ONES_ROWS = 16
LOG2E = 1.4426950408889634
PAGE_RING = 32


def _nt_dot(a, b, precision=None):
    return lax.dot_general(a, b, (((1,), (1,)), ((), ())), precision=precision,
                           preferred_element_type=F32)


def _dot(a, b, precision=None):
    return jnp.dot(a, b, precision=precision, preferred_element_type=F32)


def _t5_bucket(dist):
    dist = jnp.maximum(dist, 0)
    max_exact = N_BUCKETS // 2
    d = jnp.maximum(dist, 1).astype(F32)
    scaled = jnp.log(d / max_exact) / math.log(MAX_DISTANCE / max_exact) * (N_BUCKETS - max_exact)
    large = max_exact + jnp.floor(jnp.maximum(scaled, 0.0)).astype(I32)
    large = jnp.minimum(large, N_BUCKETS - 1)
    return jnp.where(dist < max_exact, dist, large)


def _bucket_lookup(table, bucket):
    out = jnp.zeros((table.shape[0],) + bucket.shape, F32)
    for b in range(table.shape[1]):
        out = jnp.where(bucket[None] == b, table[:, b].reshape((-1,) + (1,) * bucket.ndim), out)
    return out


def _key_to_float(key):
    bits = jnp.where(key >= 0, key, jnp.int32(INT_MIN) - key)
    return lax.bitcast_convert_type(bits, F32)


NEG_INF_KEY = INT_MIN + 0x00800000


def _proj_body(x_ref, *refs, plan, precision):
    n_w = len(plan)
    w_refs, o_refs = refs[:n_w], refs[n_w:]
    x = x_ref[...].astype(w_refs[0].dtype)
    oi = 0
    for (channel_major, outs), w_ref in zip(plan, w_refs):
        y = _nt_dot(w_ref[...], x, precision) if channel_major else _nt_dot(x, w_ref[...], precision)
        for lo, hi, scale, dtype in outs:
            part = y[lo:hi, :] if channel_major else y[:, lo:hi]
            if scale != 1.0:
                part = part * scale
            if channel_major:
                o_refs[oi][0] = part.astype(dtype)
            else:
                o_refs[oi][...] = part.astype(dtype)
            oi += 1


def _project(x2d, weights_t, plan, batch, seq, tm, precision=None):
    n, d = x2d.shape
    assert n == batch * seq and seq % tm == 0
    nt = seq // tm
    in_specs = [pl.BlockSpec((tm, d), lambda b, t: (b * nt + t, 0))]
    for w in weights_t:
        in_specs.append(pl.BlockSpec(w.shape, lambda b, t: (0, 0)))
    out_shapes, out_specs = [], []
    for (channel_major, outs) in plan:
        for lo, hi, _, dtype in outs:
            if channel_major:
                out_shapes.append(jax.ShapeDtypeStruct((batch, hi - lo, seq), dtype))
                out_specs.append(pl.BlockSpec((1, hi - lo, tm), lambda b, t: (b, 0, t)))
            else:
                out_shapes.append(jax.ShapeDtypeStruct((n, hi - lo), dtype))
                out_specs.append(pl.BlockSpec((tm, hi - lo), lambda b, t: (b * nt + t, 0)))
    return pl.pallas_call(
        functools.partial(_proj_body, plan=plan, precision=precision),
        grid=(batch, nt),
        in_specs=in_specs,
        out_specs=out_specs,
        out_shape=out_shapes,
        compiler_params=pltpu.CompilerParams(dimension_semantics=("arbitrary", "arbitrary"),
                                             vmem_limit_bytes=VMEM_LIMIT),
        name="in_proj",
    )(x2d, *weights_t)


FOLD_ROWS = 8 * SUBLANES


def _fold_rows(acc, x, op):
    for r in range(0, x.shape[0], FOLD_ROWS):
        acc = op(acc, x[r:r + FOLD_ROWS])
    return acc


def _count_rows(keys_ref, nbig, pred):
    tq = keys_ref.shape[1]

    def body(j, acc):
        off = pl.multiple_of(j * KB, KB)
        m = pred(keys_ref[pl.ds(off, KB), :], off)
        return _fold_rows(acc, jnp.where(m, 1, 0).astype(I32), jnp.add)

    acc = lax.fori_loop(0, nbig, body, jnp.zeros((FOLD_ROWS, tq), I32))
    return jnp.sum(acc, axis=0, keepdims=True)


def _select_threshold(sc_ref, nbig, topk, idx_bits):
    tq = sc_ref.shape[1]
    c0 = _count_rows(sc_ref, nbig, lambda s, off: s >= 0.0)
    nonneg = c0 >= topk
    prefix = jnp.where(nonneg, 0, INT_MIN).astype(I32)
    cge = jnp.where(nonneg, c0, nbig * KB).astype(I32)

    def bit_body(it, carry):
        prefix, cge = carry
        cand = prefix | lax.shift_left(jnp.int32(1), 30 - it)
        cand_f = _key_to_float(cand)
        c = _count_rows(sc_ref, nbig, lambda s, off: s >= cand_f)
        ok = c >= topk
        return jnp.where(ok, cand, prefix), jnp.where(ok, c, cge)

    thr_key, cge = lax.fori_loop(0, 31, bit_body, (prefix, cge))
    real = thr_key > NEG_INF_KEY
    thr = jnp.where(real, _key_to_float(thr_key), NEG_INF)
    need = jnp.logical_and(cge > topk, real)
    j_default = jnp.where(real, jnp.int32(2 ** 30), jnp.int32(-1))

    def tie_fn():
        cgt = _count_rows(sc_ref, nbig, lambda s, off: s > thr)
        want = topk - cgt

        def body(it, p):
            cand = p | lax.shift_left(jnp.int32(1), idx_bits - 1 - it)

            def pred(s, off):
                kidx = off + lax.broadcasted_iota(I32, (KB, 1), 0)
                return jnp.logical_and(s == thr, kidx < cand)

            c = _count_rows(sc_ref, nbig, pred)
            return jnp.where(c < want, cand, p)

        p = lax.fori_loop(0, idx_bits, body, jnp.zeros((1, tq), I32))
        return jnp.where(need, p, j_default)

    any_need = jnp.max(jnp.where(need, 1, 0)) > 0
    jcut = lax.cond(any_need, tie_fn, lambda: j_default)
    return thr, jcut


def _prompt_attn_body(q_ref, qi_ref, wt_ref, k_ref, ki_ref, vt_ref, bias_ref, o_ref,
                      sc_ref, madd_ref, qm_ref, qim_ref, lga_ref, lgb_ref, *, topk, idx_bits):
    i = pl.program_id(1)
    tq = q_ref.shape[0]
    per_big = KB // KC
    nbig = (i + per_big) // per_big

    lane_hi = lax.broadcasted_iota(I32, (1, LANES), 1) >= HEAD_DIM
    for h in range(N_HEADS):
        hp, sub = divmod(h, 2)
        keep = lane_hi if sub == 1 else jnp.logical_not(lane_hi)
        sl = slice(hp * LANES, (hp + 1) * LANES)
        qm_ref[h] = jnp.where(keep, q_ref[:, sl], jnp.zeros((), BF16))
        qim_ref[h] = jnp.where(keep, qi_ref[:, sl], jnp.zeros((), BF16))

    wt = wt_ref[0]
    qidx = i * tq + lax.broadcasted_iota(I32, (1, tq), 1)

    def score_body(jb, _):
        for u in range(per_big):
            off = pl.multiple_of(jb * KB + u * KC, KC)
            kc = ki_ref[pl.ds(off, KC), :]
            s = jnp.zeros((KC, tq), F32)
            for h in range(N_IDX_HEADS):
                d = _nt_dot(kc, qim_ref[h])
                s = s + wt[h:h + 1, :] * jnp.maximum(d, 0.0)
            kidx = off + lax.broadcasted_iota(I32, (KC, 1), 0)
            sc_ref[pl.ds(off, KC), :] = jnp.where(kidx <= qidx, s, NEG_INF)
        return 0

    lax.fori_loop(0, nbig, score_body, 0)

    thr, jcut = _select_threshold(sc_ref, nbig, topk, idx_bits)

    def madd_body(jb, _):
        off = pl.multiple_of(jb * KB, KB)
        k = sc_ref[pl.ds(off, KB), :]
        kidx = off + lax.broadcasted_iota(I32, (KB, 1), 0)
        sel = jnp.logical_or(k > thr, jnp.logical_and(k == thr, kidx <= jcut))
        madd_ref[pl.ds(off, KB), :] = jnp.where(sel, 0.0, NEG_INF).astype(F32)
        return 0

    lax.fori_loop(0, nbig, madd_body, 0)

    ones_rows = jnp.ones((ONES_ROWS, KB), BF16)
    zero_acc = jnp.zeros((HEAD_DIM + ONES_ROWS, tq), F32)

    def logits_step(h, lg_ref, jb):
        off = pl.multiple_of(jb * KB, KB)
        loff = pl.multiple_of((h // 2) * LANES, LANES)
        lg_ref[pl.ds(off, KB), :] = (_nt_dot(k_ref[pl.ds(off, KB), pl.ds(loff, LANES)], qm_ref[h]) * LOG2E
                                     + madd_ref[pl.ds(off, KB), :])

    def finish_logits(h, lg_ref):
        diag = pl.ds(pl.multiple_of(i * KC, KC), KC)
        lg_ref[diag, :] = lg_ref[diag, :] + bias_ref[0, h]

        @pl.when(i >= 1)
        def _():
            prev = pl.ds(pl.multiple_of((i - 1) * KC, KC), KC)
            lg_ref[prev, :] = lg_ref[prev, :] + bias_ref[1, h]

        def max_step(jb, mx):
            return _fold_rows(mx, lg_ref[pl.ds(pl.multiple_of(jb * KB, KB), KB), :], jnp.maximum)

        mx = lax.fori_loop(0, nbig, max_step, jnp.full((FOLD_ROWS, tq), NEG_INF, F32))
        return jnp.max(mx, axis=0, keepdims=True)

    def pv_step(h, lg_ref, m, jb, acc):
        off = pl.multiple_of(jb * KB, KB)
        p = jnp.exp2(lg_ref[pl.ds(off, KB), :] - m)
        vrow = pl.multiple_of(h * HEAD_DIM, HEAD_DIM)
        vsl = jnp.concatenate([vt_ref[0, pl.ds(vrow, HEAD_DIM), pl.ds(off, KB)], ones_rows], axis=0)
        return acc + _dot(vsl, p.astype(BF16))

    def write_out(h, acc):
        vrow = pl.multiple_of(h * HEAD_DIM, HEAD_DIM)
        o_ref[pl.ds(vrow, HEAD_DIM), :] = (acc[:HEAD_DIM] / acc[HEAD_DIM:HEAD_DIM + 1]).astype(o_ref.dtype)

    def advance(h, lg_new, lg_old, m_old):
        def fused_step(jb, acc):
            logits_step(h, lg_new, jb)
            return pv_step(h - 1, lg_old, m_old, jb, acc)

        write_out(h - 1, lax.fori_loop(0, nbig, fused_step, zero_acc))
        return finish_logits(h, lg_new)

    def first_logits(jb, _):
        logits_step(0, lga_ref, jb)
        return 0

    lax.fori_loop(0, nbig, first_logits, 0)

    def pair_body(t, m_even):
        m_odd = advance(2 * t + 1, lgb_ref, lga_ref, m_even)
        return advance(2 * t + 2, lga_ref, lgb_ref, m_odd)

    m_even = lax.fori_loop(0, N_HEADS // 2 - 1, pair_body, finish_logits(0, lga_ref))
    m_last = advance(N_HEADS - 1, lgb_ref, lga_ref, m_even)
    write_out(N_HEADS - 1, lax.fori_loop(
        0, nbig, lambda jb, acc: pv_step(N_HEADS - 1, lgb_ref, m_last, jb, acc), zero_acc))


def _prompt_attention(qb, qib, wit, kb, ki2b, vt, bias_t, batch, seq):
    assert seq % KB == 0 and KB % KC == 0 and TQ == KC
    nq = seq // TQ
    topk = min(TOPK_MAX, seq // 4)
    idx_bits = max(1, (seq - 1).bit_length())
    n = batch * seq
    return pl.pallas_call(
        functools.partial(_prompt_attn_body, topk=topk, idx_bits=idx_bits),
        grid=(batch, nq),
        in_specs=[
            pl.BlockSpec((TQ, D_ATT), lambda b, i: (b * nq + i, 0)),
            pl.BlockSpec((TQ, N_IDX_HEADS * IDX_DIM), lambda b, i: (b * nq + i, 0)),
            pl.BlockSpec((1, N_IDX_HEADS, TQ), lambda b, i: (b, 0, i)),
            pl.BlockSpec((seq, D_ATT), lambda b, i: (b, 0)),
            pl.BlockSpec((seq, 2 * IDX_DIM), lambda b, i: (b, 0)),
            pl.BlockSpec((1, D_ATT, seq), lambda b, i: (b, 0, 0)),
            pl.BlockSpec(bias_t.shape, lambda b, i: (0, 0, 0, 0)),
        ],
        out_specs=pl.BlockSpec((D_ATT, TQ), lambda b, i: (0, b * nq + i)),
        out_shape=jax.ShapeDtypeStruct((D_ATT, n), BF16),
        scratch_shapes=[
            pltpu.VMEM((seq, TQ), F32),
            pltpu.VMEM((seq, TQ), F32),
            pltpu.VMEM((N_HEADS, TQ, LANES), BF16),
            pltpu.VMEM((N_IDX_HEADS, TQ, LANES), BF16),
            pltpu.VMEM((seq, TQ), F32),
            pltpu.VMEM((seq, TQ), F32),
        ],
        compiler_params=pltpu.CompilerParams(dimension_semantics=("arbitrary", "arbitrary"),
                                             vmem_limit_bytes=VMEM_LIMIT),
        name="prompt_attention",
    )(qb, qib, wit, kb, ki2b, vt, bias_t)


def _prompt_bias_tiles(rel_bias):
    shifted = (rel_bias - rel_bias[N_BUCKETS - 1:N_BUCKETS]).T.astype(F32)
    c = jnp.arange(KC, dtype=I32)[:, None]
    r = jnp.arange(TQ, dtype=I32)[None, :]
    tiles = []
    for base in (0, TQ):
        dist = base + r - c
        bucket = jnp.where(dist >= 0, _t5_bucket(dist), N_BUCKETS - 1)
        tiles.append(_bucket_lookup(shifted, bucket))
    return jnp.stack(tiles) * LOG2E


def _lru_gates(xc, wa_ref, ba_ref, wx_ref, bx_ref, lam_ref, precision):
    xm = xc.astype(wa_ref.dtype)
    r = jax.nn.sigmoid(_dot(xm, wa_ref[...], precision) + ba_ref[...])
    g = jax.nn.sigmoid(_dot(xm, wx_ref[...], precision) + bx_ref[...])
    log_a = -LRU_C * r * jax.nn.softplus(-lam_ref[...])
    a = jnp.exp(log_a)
    u = jnp.sqrt(-jnp.tanh(log_a) * (a * a + 1.0)) * g * xc
    return a, u


def _rglru_prompt_body(xr_ref, gr_ref, cw_ref, cb_ref, wa_ref, ba_ref, wx_ref, bx_ref, lam_ref,
                       rnn_ref, hlast_ref, ext_ref, hc_ref):
    tt = xr_ref.shape[0]

    @pl.when(pl.program_id(1) == 0)
    def _():
        ext_ref[0:SUBLANES, :] = jnp.zeros((SUBLANES, ext_ref.shape[1]), F32)
        hc_ref[...] = jnp.zeros_like(hc_ref)

    x = xr_ref[...]
    ext_ref[SUBLANES:SUBLANES + tt, :] = x
    cw = cw_ref[...]
    xc = cb_ref[...] + ext_ref[SUBLANES - 3:SUBLANES - 3 + tt, :] * cw[0:1]
    for j in range(1, CONV_W):
        s = CONV_W - 1 - j
        xc = xc + ext_ref[SUBLANES - s:SUBLANES - s + tt, :] * cw[j:j + 1]
    ext_ref[0:SUBLANES, :] = x[tt - SUBLANES:tt]

    a, u = _lru_gates(xc, wa_ref, ba_ref, wx_ref, bx_ref, lam_ref, None)
    row = lax.broadcasted_iota(I32, (tt, 1), 0)
    d = 1
    while d < tt:
        keep = row >= d
        a_sh = jnp.where(keep, pltpu.roll(a, d, 0), 1.0)
        u_sh = jnp.where(keep, pltpu.roll(u, d, 0), 0.0)
        u = u + a * u_sh
        a = a * a_sh
        d *= 2
    h = u + a * hc_ref[...]
    h_last = h[tt - 1:tt]
    hc_ref[...] = h_last
    hlast_ref[0] = h_last
    rnn_ref[...] = (h * jax.nn.gelu(gr_ref[...])).astype(rnn_ref.dtype)


def _rglru_prompt(xr, gr, cw, cb, wa_bd, ba, wx_bd, bx, lam, batch, seq, tt):
    n, dr = xr.shape
    nt = seq // tt
    full = lambda shape: pl.BlockSpec(shape, lambda b, t: (0,) * len(shape))
    return pl.pallas_call(
        _rglru_prompt_body,
        grid=(batch, nt),
        in_specs=[
            pl.BlockSpec((tt, dr), lambda b, t: (b * nt + t, 0)),
            pl.BlockSpec((tt, dr), lambda b, t: (b * nt + t, 0)),
            full(cw.shape), full(cb.shape), full(wa_bd.shape), full(ba.shape),
            full(wx_bd.shape), full(bx.shape), full(lam.shape),
        ],
        out_specs=[
            pl.BlockSpec((tt, dr), lambda b, t: (b * nt + t, 0)),
            pl.BlockSpec((1, 1, dr), lambda b, t: (b, 0, 0)),
        ],
        out_shape=[jax.ShapeDtypeStruct((n, dr), BF16), jax.ShapeDtypeStruct((batch, 1, dr), F32)],
        scratch_shapes=[pltpu.VMEM((tt + SUBLANES, dr), F32), pltpu.VMEM((1, dr), F32)],
        compiler_params=pltpu.CompilerParams(dimension_semantics=("arbitrary", "arbitrary"),
                                             vmem_limit_bytes=VMEM_LIMIT),
        name="rglru_prompt",
    )(xr, gr, cw, cb, wa_bd, ba, wx_bd, bx, lam)


def _rglru_sample_body(xr_ref, gr_ref, cp_ref, hp_ref, cw_ref, cb_ref, wa_ref, ba_ref, wx_ref, bx_ref,
                       lam_ref, rnn_ref, h_ref):
    cw = cw_ref[...]
    xc = cb_ref[...] + cp_ref[0] * cw[0:1]
    for j in range(1, CONV_W - 1):
        xc = xc + cp_ref[j] * cw[j:j + 1]
    xc = xc + xr_ref[...] * cw[CONV_W - 1:CONV_W]
    a, u = _lru_gates(xc, wa_ref, ba_ref, wx_ref, bx_ref, lam_ref, lax.Precision.HIGHEST)
    h = u + a * hp_ref[...]
    h_ref[...] = h
    rnn_ref[...] = h * jax.nn.gelu(gr_ref[...])


def _rglru_sample(xr, gr, conv_prev_t, h_prev, cw, cb, wa_bd, ba, wx_bd, bx, lam):
    n, dr = xr.shape
    return pl.pallas_call(
        _rglru_sample_body,
        out_shape=[jax.ShapeDtypeStruct((n, dr), F32), jax.ShapeDtypeStruct((n, dr), F32)],
        compiler_params=pltpu.CompilerParams(vmem_limit_bytes=VMEM_LIMIT),
        name="rglru_sample",
    )(xr, gr, conv_prev_t, h_prev, cw, cb, wa_bd, ba, wx_bd, bx, lam)


def _layer_norm(x, g, b):
    mu = jnp.mean(x, axis=-1, keepdims=True)
    xc = x - mu
    var = jnp.mean(jnp.square(xc), axis=-1, keepdims=True)
    return xc * lax.rsqrt(var + LN_EPS) * g + b


def _mlp_body(attn_ref, rnn_ref, x_ref, woa_ref, wor_ref, g1_ref, b1_ref, wup_ref, bup_ref,
              wdn_ref, bdn_ref, g2_ref, b2_ref, y_ref, *, alpha, attn_channel_major, ff_chunk):
    cdt = woa_ref.dtype
    if attn_channel_major:
        mix = lax.dot_general(attn_ref[...].astype(cdt), woa_ref[...], (((0,), (0,)), ((), ())),
                              preferred_element_type=F32)
    else:
        mix = _dot(attn_ref[...].astype(cdt), woa_ref[...])
    mix = mix + _dot(rnn_ref[...].astype(cdt), wor_ref[...])
    x1 = _layer_norm(alpha * x_ref[...] + mix, g1_ref[...], b1_ref[...])
    x1c = x1.astype(cdt)
    d_ff = wup_ref.shape[1]
    y = None
    for c in range(0, d_ff, ff_chunk):
        hid = _dot(x1c, wup_ref[:, c:c + ff_chunk]) + bup_ref[:, c:c + ff_chunk]
        hid = jnp.square(jnp.maximum(hid, 0.0)).astype(cdt)
        part = _dot(hid, wdn_ref[c:c + ff_chunk, :])
        y = part if y is None else y + part
    y_ref[...] = _layer_norm(alpha * x1 + y + bdn_ref[...], g2_ref[...], b2_ref[...])


def _out_mlp(attn, rnn, x2d, woa, wor, g1, b1, wup, bup, wdn, bdn, g2, b2, *, alpha, tm,
             attn_channel_major):
    n, dm = x2d.shape
    da = woa.shape[0]
    full = lambda a: pl.BlockSpec(a.shape, lambda i: (0,) * a.ndim)
    attn_spec = (pl.BlockSpec((da, tm), lambda i: (0, i)) if attn_channel_major
                 else pl.BlockSpec((tm, da), lambda i: (i, 0)))
    consts = (woa, wor, g1, b1, wup, bup, wdn, bdn, g2, b2)
    return pl.pallas_call(
        functools.partial(_mlp_body, alpha=alpha, attn_channel_major=attn_channel_major, ff_chunk=1024),
        grid=(n // tm,),
        in_specs=[attn_spec, pl.BlockSpec((tm, rnn.shape[1]), lambda i: (i, 0)),
                  pl.BlockSpec((tm, dm), lambda i: (i, 0))] + [full(a) for a in consts],
        out_specs=pl.BlockSpec((tm, dm), lambda i: (i, 0)),
        out_shape=jax.ShapeDtypeStruct((n, dm), F32),
        compiler_params=pltpu.CompilerParams(dimension_semantics=("arbitrary",),
                                             vmem_limit_bytes=VMEM_LIMIT),
        name="out_mlp",
    )(attn, rnn, x2d, *consts)


def _sample_keys_body(pt_ref, qi_ref, w_ref, kin_ref, cki_ref, keys_ref, knew_ref, kibuf, sem,
                      *, layer):
    b = pl.program_id(0)
    nb = pl.num_programs(0)
    n_pages = kibuf.shape[1]

    def page_copy(bb, slot, p):
        return pltpu.make_async_copy(cki_ref.at[layer, pt_ref[bb, p]], kibuf.at[slot, p], sem.at[slot])

    def issue_all(bb, slot):
        def body(p, _):
            page_copy(bb, slot, p).start()
            return 0
        lax.fori_loop(0, n_pages, body, 0)

    slot = lax.rem(b, 2)

    @pl.when(b == 0)
    def _():
        issue_all(b, slot)

    @pl.when(b + 1 < nb)
    def _():
        issue_all(b + 1, 1 - slot)

    def wait_body(p, _):
        page_copy(b, slot, p).wait()
        return 0

    lax.fori_loop(0, n_pages, wait_body, 0)

    qi16 = qi_ref[0].astype(BF16)
    w = w_ref[0]
    group = 16

    def score_body(g, _):
        for u in range(group):
            p = g * group + u
            d = _dot(qi16, kibuf[slot, p].astype(BF16))
            keys_ref[0, pl.ds(p, 1), :] = jnp.sum(w * jnp.maximum(d, 0.0), axis=0, keepdims=True)
        return 0

    lax.fori_loop(0, n_pages // group, score_body, 0)

    dn = jnp.sum(qi16.astype(F32) * kin_ref[0].astype(BF16).astype(F32), axis=1, keepdims=True)
    score_new = jnp.sum(w * jnp.maximum(dn, 0.0), axis=0, keepdims=True)
    knew_ref[0] = jnp.broadcast_to(score_new, (1, LANES))


def _sample_keys(page_table, qi_s, wi_col, ki_new, cache_ki_t, layer):
    bd, n_pages = page_table.shape
    assert n_pages % 16 == 0
    grid_spec = pltpu.PrefetchScalarGridSpec(
        num_scalar_prefetch=1,
        grid=(bd,),
        in_specs=[
            pl.BlockSpec((1, N_IDX_HEADS, IDX_DIM), lambda b, pt: (b, 0, 0)),
            pl.BlockSpec((1, N_IDX_HEADS, 1), lambda b, pt: (b, 0, 0)),
            pl.BlockSpec((1, 1, IDX_DIM), lambda b, pt: (b, 0, 0)),
            pl.BlockSpec(memory_space=pl.ANY),
        ],
        out_specs=[
            pl.BlockSpec((1, n_pages, PAGE_SIZE), lambda b, pt: (b, 0, 0)),
            pl.BlockSpec((1, 1, LANES), lambda b, pt: (b, 0, 0)),
        ],
        scratch_shapes=[
            pltpu.VMEM((2, n_pages, IDX_DIM, PAGE_SIZE), F32),
            pltpu.SemaphoreType.DMA((2,)),
        ],
    )
    return pl.pallas_call(
        functools.partial(_sample_keys_body, layer=layer),
        grid_spec=grid_spec,
        out_shape=[jax.ShapeDtypeStruct((bd, n_pages, PAGE_SIZE), F32),
                   jax.ShapeDtypeStruct((bd, 1, LANES), F32)],
        compiler_params=pltpu.CompilerParams(dimension_semantics=("arbitrary",),
                                             vmem_limit_bytes=VMEM_LIMIT),
        name="sample_keys",
    )(page_table, qi_s, wi_col, ki_new, cache_ki_t)


def _sample_mask_body(keys_ref, knew_ref, madd_ref, maddn_ref, *, topk, idx_bits):
    keys = keys_ref[...]
    key_new = knew_ref[:, :, 0:1]
    n_past = keys.shape[1] * keys.shape[2]
    kidx = (lax.broadcasted_iota(I32, keys.shape, 1) * PAGE_SIZE
            + lax.broadcasted_iota(I32, keys.shape, 2))

    def count(pred_past, pred_new):
        c = jnp.sum(jnp.where(pred_past, 1, 0).astype(I32), axis=1, keepdims=True)
        return jnp.sum(c, axis=2, keepdims=True) + jnp.where(pred_new, 1, 0).astype(I32)

    c0 = count(keys >= 0.0, key_new >= 0.0)
    nonneg = c0 >= topk
    prefix = jnp.where(nonneg, 0, INT_MIN).astype(I32)
    cge = jnp.where(nonneg, c0, n_past + 1).astype(I32)

    def bit_body(it, carry):
        prefix, cge = carry
        cand = prefix | lax.shift_left(jnp.int32(1), 30 - it)
        cand_f = _key_to_float(cand)
        c = count(keys >= cand_f, key_new >= cand_f)
        ok = c >= topk
        return jnp.where(ok, cand, prefix), jnp.where(ok, c, cge)

    thr_key, cge = lax.fori_loop(0, 31, bit_body, (prefix, cge))
    real = thr_key > NEG_INF_KEY
    thr = jnp.where(real, _key_to_float(thr_key), NEG_INF)
    need = jnp.logical_and(cge > topk, real)
    j_default = jnp.where(real, jnp.int32(2 ** 30), jnp.int32(-1))

    def tie_fn():
        want = topk - count(keys > thr, key_new > thr)

        def body(it, p):
            cand = p | lax.shift_left(jnp.int32(1), idx_bits - 1 - it)
            c = count(jnp.logical_and(keys == thr, kidx < cand),
                      jnp.logical_and(key_new == thr, n_past < cand))
            return jnp.where(c < want, cand, p)

        p = lax.fori_loop(0, idx_bits, body, jnp.zeros(thr_key.shape, I32))
        return jnp.where(need, p, j_default)

    jcut = lax.cond(jnp.max(jnp.where(need, 1, 0)) > 0, tie_fn, lambda: j_default)
    sel = jnp.logical_or(keys > thr, jnp.logical_and(keys == thr, kidx <= jcut))
    sel_new = jnp.logical_or(key_new > thr, jnp.logical_and(key_new == thr, n_past <= jcut))
    madd_ref[...] = jnp.where(sel, 0.0, NEG_INF).astype(F32)
    maddn_ref[...] = jnp.broadcast_to(jnp.where(sel_new, 0.0, NEG_INF).astype(F32), maddn_ref.shape)


def _sample_mask(keys, key_new, topk):
    bd, n_pages, _ = keys.shape
    idx_bits = max(1, (n_pages * PAGE_SIZE).bit_length())
    return pl.pallas_call(
        functools.partial(_sample_mask_body, topk=topk, idx_bits=idx_bits),
        out_shape=[jax.ShapeDtypeStruct(keys.shape, F32), jax.ShapeDtypeStruct(key_new.shape, F32)],
        compiler_params=pltpu.CompilerParams(vmem_limit_bytes=VMEM_LIMIT),
        name="sample_mask",
    )(keys, key_new)


def _sample_attend_body(pt_ref, qb_ref, q_ref, kn_ref, vn_ref, madd_ref, maddn_ref, btab_ref,
                        ck_ref, cv_ref, o_ref, ring_ref, lg_ref, acc_ref, sem, *, layer):
    b = pl.program_id(0)
    n_pages = madd_ref.shape[1]
    depth = ring_ref.shape[0]
    group = 4
    per_seq = 2 * n_pages
    total_items = pl.num_programs(0) * per_seq
    base = b * per_seq

    def item_copy(t, cache_ref, seq_i, page_i):
        slot = lax.rem(t, depth)
        return pltpu.make_async_copy(cache_ref.at[layer, pt_ref[seq_i, page_i]], ring_ref.at[slot],
                                     sem.at[slot])

    def start_item(t):
        seq_i = lax.div(t, per_seq)
        r = t - seq_i * per_seq
        is_v = r >= n_pages

        @pl.when(jnp.logical_not(is_v))
        def _():
            item_copy(t, ck_ref, seq_i, r).start()

        @pl.when(is_v)
        def _():
            item_copy(t, cv_ref, seq_i, r - n_pages).start()

    def finish_item(t):
        item_copy(t, ck_ref, 0, 0).wait()

    def refill(t):
        @pl.when(t + depth < total_items)
        def _():
            start_item(t + depth)

    @pl.when(b == 0)
    def _():
        def prime(t, _):
            start_item(t)
            return 0
        lax.fori_loop(0, depth, prime, 0)

    def k_body(g, _):
        for u in range(group):
            p = g * group + u
            t = base + p
            finish_item(t)
            slot = lax.rem(t, depth)
            mrow = madd_ref[0, pl.ds(p, 1), :]
            is_last = p == n_pages - 1
            for h in range(N_HEADS):
                lg = jnp.sum(ring_ref[slot, h] * qb_ref[0, h], axis=0, keepdims=True) + mrow
                lg_ref[h, pl.ds(p, 1), :] = lg + jnp.where(is_last, btab_ref[h:h + 1, 0:PAGE_SIZE], 0.0)
            refill(t)
        return 0

    lax.fori_loop(0, n_pages // group, k_body, 0)

    def total(x, op):
        return op(op(x, axis=0, keepdims=True), axis=1, keepdims=True)

    lg_new = (jnp.sum(q_ref[0] * kn_ref[0], axis=1, keepdims=True)
              + btab_ref[:, PAGE_SIZE:PAGE_SIZE + 1] + maddn_ref[0][:, 0:1])
    p_new, l = [], []
    for h in range(N_HEADS):
        lgh = lg_ref[h]
        mh = jnp.maximum(total(lgh, jnp.max), lg_new[h:h + 1])
        prh = jnp.exp(lgh - mh)
        lg_ref[h] = prh
        p_new.append(jnp.exp(lg_new[h:h + 1] - mh))
        l.append(total(prh, jnp.sum) + p_new[h])

    acc_ref[...] = jnp.zeros_like(acc_ref)

    def v_body(g, _):
        parts = [None] * N_HEADS
        for u in range(group):
            p = g * group + u
            t = base + n_pages + p
            finish_item(t)
            slot = lax.rem(t, depth)
            for h in range(N_HEADS):
                term = ring_ref[slot, h] * lg_ref[h, pl.ds(p, 1), :]
                parts[h] = term if parts[h] is None else parts[h] + term
            refill(t)
        for h in range(N_HEADS):
            acc_ref[h] += parts[h]
        return 0

    lax.fori_loop(0, n_pages // group, v_body, 0)

    for h in range(N_HEADS):
        past = jnp.sum(acc_ref[h].T, axis=0, keepdims=True)
        o_ref[0, h:h + 1, :] = (past + p_new[h] * vn_ref[0, h:h + 1, :]) / l[h]


def _sample_attend(page_table, q_lanes, q_s, k_new, v_new, madd, madd_new, btab, cache_k_t, cache_v_t,
                   layer):
    bd, n_pages = page_table.shape
    assert n_pages % 4 == 0 and 2 * n_pages >= PAGE_RING
    blk3 = lambda b, pt: (b, 0, 0)
    grid_spec = pltpu.PrefetchScalarGridSpec(
        num_scalar_prefetch=1,
        grid=(bd,),
        in_specs=[
            pl.BlockSpec((1, N_HEADS, HEAD_DIM, PAGE_SIZE), lambda b, pt: (b, 0, 0, 0)),
            pl.BlockSpec((1, N_HEADS, HEAD_DIM), blk3),
            pl.BlockSpec((1, N_HEADS, HEAD_DIM), blk3),
            pl.BlockSpec((1, N_HEADS, HEAD_DIM), blk3),
            pl.BlockSpec((1, n_pages, PAGE_SIZE), blk3),
            pl.BlockSpec((1, 1, LANES), blk3),
            pl.BlockSpec(btab.shape, lambda b, pt: (0, 0)),
            pl.BlockSpec(memory_space=pl.ANY),
            pl.BlockSpec(memory_space=pl.ANY),
        ],
        out_specs=pl.BlockSpec((1, N_HEADS, HEAD_DIM), blk3),
        scratch_shapes=[
            pltpu.VMEM((PAGE_RING, N_HEADS, HEAD_DIM, PAGE_SIZE), F32),
            pltpu.VMEM((N_HEADS, n_pages, PAGE_SIZE), F32),
            pltpu.VMEM((N_HEADS, HEAD_DIM, PAGE_SIZE), F32),
            pltpu.SemaphoreType.DMA((PAGE_RING,)),
        ],
    )
    return pl.pallas_call(
        functools.partial(_sample_attend_body, layer=layer),
        grid_spec=grid_spec,
        out_shape=jax.ShapeDtypeStruct((bd, N_HEADS, HEAD_DIM), F32),
        compiler_params=pltpu.CompilerParams(dimension_semantics=("arbitrary",),
                                             vmem_limit_bytes=VMEM_LIMIT),
        name="sample_attend",
    )(page_table, q_lanes, q_s, k_new, v_new, madd, madd_new, btab, cache_k_t, cache_v_t)


def _block_diag(w):
    nb, c, d = w.shape
    eye = jnp.eye(nb, dtype=w.dtype)
    return (eye[:, None, :, None] * w[:, :, None, :]).reshape(nb * c, nb * d)


def kernel(x_prompt, x_sample, cache_k, cache_v, cache_k_idx, state_h, state_conv, page_table,
           rel_bias, w_in, conv_w, conv_b, w_a, b_a, w_x, b_x, lru_lambda, w_out,
           ln1_g, ln1_b, w_up, b_up, w_down, b_down, ln2_g, ln2_b):
    depth = w_in.shape[0]
    batch, seq, dm = x_prompt.shape
    bd, dec_seq, _ = x_sample.shape
    assert dec_seq == 1, "the decode pass handles one new token per sequence"
    d_rnn = dm - D_ATT
    alpha = (2 * depth) ** 0.25
    n_pages = page_table.shape[1]
    n_past = n_pages * PAGE_SIZE
    topk_s = min(TOPK_MAX, (n_past + dec_seq) // 4)
    row = lambda a: a.reshape(1, -1)
    c16 = lambda a: a.astype(BF16)

    sizes = (D_ATT, D_ATT, D_ATT, N_IDX_HEADS * IDX_DIM, IDX_DIM, N_IDX_HEADS, d_rnn, d_rnn)
    offs = [0]
    for s in sizes:
        offs.append(offs[-1] + s)

    w_in_t = jnp.swapaxes(w_in, 1, 2)
    cache_ki_t = jnp.swapaxes(cache_k_idx, 2, 3)
    cache_k_t = jnp.transpose(cache_k, (0, 1, 3, 4, 2))
    cache_v_t = jnp.transpose(cache_v, (0, 1, 3, 4, 2))

    bias_t = _prompt_bias_tiles(rel_bias)
    shifted = (rel_bias - rel_bias[N_BUCKETS - 1:N_BUCKETS]).T.astype(F32)
    btab = jnp.concatenate([
        _bucket_lookup(shifted, _t5_bucket(PAGE_SIZE - jnp.arange(PAGE_SIZE, dtype=I32))),
        _bucket_lookup(shifted, _t5_bucket(jnp.zeros((PAGE_SIZE,), I32)))], axis=1)

    yp = x_prompt.reshape(batch * seq, dm)
    ys = x_sample.reshape(bd, dm)
    outs = [[] for _ in range(10)]
    for layer in range(depth):
        wt = w_in_t[layer]
        wq, wk, wv, wqi, wki, wwi, wxr, wgr = [wt[offs[j]:offs[j + 1]] for j in range(8)]
        cw, cb, lam = conv_w[layer], row(conv_b[layer]), row(lru_lambda[layer])
        wa_bd, wx_bd = _block_diag(w_a[layer]), _block_diag(w_x[layer])
        ba, bx = row(b_a[layer]), row(b_x[layer])
        woa, wor = c16(w_out[layer][:D_ATT]), c16(w_out[layer][D_ATT:])
        mlp_consts = (woa, wor, row(ln1_g[layer]), row(ln1_b[layer]), c16(w_up[layer]), row(b_up[layer]),
                      c16(w_down[layer]), row(b_down[layer]), row(ln2_g[layer]), row(ln2_b[layer]))

        p_weights = [c16(wq), c16(wk), c16(wk), c16(wv), c16(wqi), c16(jnp.concatenate([wki, wki], 0)),
                     c16(wki), c16(wwi), c16(wxr), c16(wgr)]
        p_plan = (
            (False, ((0, D_ATT, HEAD_DIM ** -0.5, BF16),)),
            (False, ((0, D_ATT, 1.0, BF16),)),
            (True, ((0, D_ATT, 1.0, F32),)),
            (True, ((0, D_ATT, 1.0, F32), (0, D_ATT, 1.0, BF16))),
            (False, ((0, N_IDX_HEADS * IDX_DIM, 1.0, BF16),)),
            (False, ((0, 2 * IDX_DIM, 1.0, BF16),)),
            (True, ((0, IDX_DIM, 1.0, F32),)),
            (True, ((0, N_IDX_HEADS, 1.0, F32),)),
            (False, ((0, d_rnn, 1.0, F32),)),
            (False, ((0, d_rnn, 1.0, F32),)),
        )
        qb, kb, k_t, v_t, vtb, qib, ki2b, ki_t, wit, xr_p, gr_p = _project(
            yp, p_weights, p_plan, batch, seq, tm=512)
        attn_t = _prompt_attention(qb, qib, wit, kb, ki2b, vtb, bias_t, batch, seq)
        rnn_p, h_p = _rglru_prompt(xr_p, gr_p, cw, cb, c16(wa_bd), ba, c16(wx_bd), bx, lam,
                                   batch, seq, tt=512)
        conv_p = xr_p.reshape(batch, seq, d_rnn)[:, seq - (CONV_W - 1):]
        yp = _out_mlp(attn_t, rnn_p, yp, *mlp_consts, alpha=alpha, tm=512, attn_channel_major=True)

        pad_wi = jnp.pad(wwi, ((0, LANES - N_IDX_HEADS), (0, 0)))
        s_weights = [wq, wk, wv, wqi, wki, pad_wi, wxr, wgr]
        s_plan = (
            (False, ((0, D_ATT, HEAD_DIM ** -0.5, F32),)),
            (False, ((0, D_ATT, 1.0, F32),)),
            (False, ((0, D_ATT, 1.0, F32),)),
            (False, ((0, N_IDX_HEADS * IDX_DIM, 1.0, F32),)),
            (False, ((0, IDX_DIM, 1.0, F32),)),
            (False, ((0, LANES, 1.0, F32),)),
            (False, ((0, d_rnn, 1.0, F32),)),
            (False, ((0, d_rnn, 1.0, F32),)),
        )
        q_s, k_s, v_s, qi_s, ki_s, wi_s, xr_s, gr_s = _project(ys, s_weights, s_plan, 1, bd, tm=bd,
                                                               precision=lax.Precision.HIGHEST)
        keys, key_new = _sample_keys(page_table, qi_s.reshape(bd, N_IDX_HEADS, IDX_DIM),
                                     wi_s[:, :N_IDX_HEADS].reshape(bd, N_IDX_HEADS, 1),
                                     ki_s.reshape(bd, 1, IDX_DIM), cache_ki_t, layer)
        madd, madd_new = _sample_mask(keys, key_new, topk_s)
        heads = lambda a: a.reshape(bd, N_HEADS, HEAD_DIM)
        q_lanes = jnp.broadcast_to(heads(q_s)[..., None], (bd, N_HEADS, HEAD_DIM, PAGE_SIZE))
        attn_s = _sample_attend(page_table, q_lanes, heads(q_s), heads(k_s), heads(v_s), madd, madd_new,
                                btab, cache_k_t, cache_v_t, layer)
        conv_prev = state_conv[layer]
        rnn_s, h_s = _rglru_sample(xr_s, gr_s, jnp.swapaxes(conv_prev, 0, 1), state_h[layer], cw, cb,
                                   wa_bd, ba, wx_bd, bx, lam)
        conv_s = jnp.concatenate([conv_prev[:, 1:], xr_s[:, None, :]], axis=1)
        ys = _out_mlp(attn_s.reshape(bd, D_ATT), rnn_s, ys, *mlp_consts, alpha=alpha, tm=bd,
                      attn_channel_major=False)

        per_head_t = lambda a: jnp.transpose(a.reshape(batch, N_HEADS, HEAD_DIM, seq), (0, 3, 1, 2))
        layer_outs = (
            per_head_t(k_t), per_head_t(v_t), jnp.swapaxes(ki_t, 1, 2), h_p.reshape(batch, d_rnn), conv_p,
            k_s.reshape(bd, dec_seq, N_HEADS, HEAD_DIM), v_s.reshape(bd, dec_seq, N_HEADS, HEAD_DIM),
            ki_s.reshape(bd, dec_seq, IDX_DIM), h_s, conv_s,
        )
        for lst, a in zip(outs, layer_outs):
            lst.append(a)

    stacked = [jnp.stack(lst) for lst in outs]
    return (yp.reshape(batch, seq, dm), ys.reshape(bd, dec_seq, dm), *stacked)
```

```python
import functools
import math

import jax
import jax.numpy as jnp
from jax import lax
from jax.experimental import pallas as pl
from jax.experimental.pallas import tpu as pltpu

F32, BF16, I32 = jnp.float32, jnp.bfloat16, jnp.int32

N_HEADS = 8
HEAD_DIM = 64
D_ATT = N_HEADS * HEAD_DIM
N_IDX_HEADS = 8
IDX_DIM = 64
TOPK_MAX = 256
N_RNN_BLOCKS = 8
CONV_W = 4
LRU_C = 8.0
N_BUCKETS = 32
MAX_DISTANCE = 128
LN_EPS = 1e-5
PAGE_SIZE = 128

LANES = 128
SUBLANES = 8
VMEM_LIMIT = 56 * 1024 * 1024

INT_MIN = -(2 ** 31)
NEG_INF = float("-inf")

TQ = 256
KC = 256
KB = 1024
KS = 512
ONES_ROWS = 16
LOG2E = 1.4426950408889634
PAGE_RING = 32


def _nt_dot(a, b, precision=None):
    return lax.dot_general(a, b, (((1,), (1,)), ((), ())), precision=precision,
                           preferred_element_type=F32)


def _dot(a, b, precision=None):
    return jnp.dot(a, b, precision=precision, preferred_element_type=F32)


def _t5_bucket(dist):
    dist = jnp.maximum(dist, 0)
    max_exact = N_BUCKETS // 2
    d = jnp.maximum(dist, 1).astype(F32)
    scaled = jnp.log(d / max_exact) / math.log(MAX_DISTANCE / max_exact) * (N_BUCKETS - max_exact)
    large = max_exact + jnp.floor(jnp.maximum(scaled, 0.0)).astype(I32)
    large = jnp.minimum(large, N_BUCKETS - 1)
    return jnp.where(dist < max_exact, dist, large)


def _bucket_lookup(table, bucket):
    out = jnp.zeros((table.shape[0],) + bucket.shape, F32)
    for b in range(table.shape[1]):
        out = jnp.where(bucket[None] == b, table[:, b].reshape((-1,) + (1,) * bucket.ndim), out)
    return out


def _key_to_float(key):
    bits = jnp.where(key >= 0, key, jnp.int32(INT_MIN) - key)
    return lax.bitcast_convert_type(bits, F32)


NEG_INF_KEY = INT_MIN + 0x00800000


def _proj_body(x_ref, *refs, plan, precision):
    n_w = len(plan)
    w_refs, o_refs = refs[:n_w], refs[n_w:]
    x = x_ref[...].astype(w_refs[0].dtype)
    oi = 0
    for (channel_major, outs), w_ref in zip(plan, w_refs):
        y = _nt_dot(w_ref[...], x, precision) if channel_major else _nt_dot(x, w_ref[...], precision)
        for lo, hi, scale, dtype in outs:
            part = y[lo:hi, :] if channel_major else y[:, lo:hi]
            if scale != 1.0:
                part = part * scale
            if channel_major:
                o_refs[oi][0] = part.astype(dtype)
            else:
                o_refs[oi][...] = part.astype(dtype)
            oi += 1


def _project(x2d, weights_t, plan, batch, seq, tm, precision=None):
    n, d = x2d.shape
    assert n == batch * seq and seq % tm == 0
    nt = seq // tm
    in_specs = [pl.BlockSpec((tm, d), lambda b, t: (b * nt + t, 0))]
    for w in weights_t:
        in_specs.append(pl.BlockSpec(w.shape, lambda b, t: (0, 0)))
    out_shapes, out_specs = [], []
    for (channel_major, outs) in plan:
        for lo, hi, _, dtype in outs:
            if channel_major:
                out_shapes.append(jax.ShapeDtypeStruct((batch, hi - lo, seq), dtype))
                out_specs.append(pl.BlockSpec((1, hi - lo, tm), lambda b, t: (b, 0, t)))
            else:
                out_shapes.append(jax.ShapeDtypeStruct((n, hi - lo), dtype))
                out_specs.append(pl.BlockSpec((tm, hi - lo), lambda b, t: (b * nt + t, 0)))
    return pl.pallas_call(
        functools.partial(_proj_body, plan=plan, precision=precision),
        grid=(batch, nt),
        in_specs=in_specs,
        out_specs=out_specs,
        out_shape=out_shapes,
        compiler_params=pltpu.CompilerParams(dimension_semantics=("arbitrary", "arbitrary"),
                                             vmem_limit_bytes=VMEM_LIMIT),
        name="in_proj",
    )(x2d, *weights_t)


FOLD_ROWS = 8 * SUBLANES


def _fold_rows(acc, x, op):
    for r in range(0, x.shape[0], FOLD_ROWS):
        acc = op(acc, x[r:r + FOLD_ROWS])
    return acc


def _count_rows(keys_ref, nsteps, pred):
    tq = keys_ref.shape[1]

    def body(j, acc):
        off = pl.multiple_of(j * KS, KS)
        m = pred(keys_ref[pl.ds(off, KS), :], off)
        return _fold_rows(acc, jnp.where(m, 1, 0).astype(I32), jnp.add)

    acc = lax.fori_loop(0, nsteps, body, jnp.zeros((FOLD_ROWS, tq), I32))
    return jnp.sum(acc, axis=0, keepdims=True)


def _select_threshold(sc_ref, nbig, topk, idx_bits):
    tq = sc_ref.shape[1]
    c0 = _count_rows(sc_ref, nbig, lambda s, off: s >= 0.0)
    nonneg = c0 >= topk
    prefix = jnp.where(nonneg, 0, INT_MIN).astype(I32)
    cge = jnp.where(nonneg, c0, nbig * KS).astype(I32)

    def bit_body(it, carry):
        prefix, cge = carry
        cand = prefix | lax.shift_left(jnp.int32(1), 30 - it)
        cand_f = _key_to_float(cand)
        c = _count_rows(sc_ref, nbig, lambda s, off: s >= cand_f)
        ok = c >= topk
        return jnp.where(ok, cand, prefix), jnp.where(ok, c, cge)

    thr_key, cge = lax.fori_loop(0, 31, bit_body, (prefix, cge))
    real = thr_key > NEG_INF_KEY
    thr = jnp.where(real, _key_to_float(thr_key), NEG_INF)
    need = jnp.logical_and(cge > topk, real)
    j_default = jnp.where(real, jnp.int32(2 ** 30), jnp.int32(-1))

    def tie_fn():
        cgt = _count_rows(sc_ref, nbig, lambda s, off: s > thr)
        want = topk - cgt

        def body(it, p):
            cand = p | lax.shift_left(jnp.int32(1), idx_bits - 1 - it)

            def pred(s, off):
                kidx = off + lax.broadcasted_iota(I32, (KS, 1), 0)
                return jnp.logical_and(s == thr, kidx < cand)

            c = _count_rows(sc_ref, nbig, pred)
            return jnp.where(c < want, cand, p)

        p = lax.fori_loop(0, idx_bits, body, jnp.zeros((1, tq), I32))
        return jnp.where(need, p, j_default)

    any_need = jnp.max(jnp.where(need, 1, 0)) > 0
    jcut = lax.cond(any_need, tie_fn, lambda: j_default)
    return thr, jcut


def _prompt_attn_body(q_ref, qi_ref, wt_ref, k_ref, ki_ref, vt_ref, bias_ref, o_ref,
                      sc_ref, madd_ref, qm_ref, qim_ref, lga_ref, lgb_ref, *, topk, idx_bits):
    i = pl.program_id(1)
    tq = q_ref.shape[0]
    per_big = KB // KC
    nbig = (i + per_big) // per_big

    lane_hi = lax.broadcasted_iota(I32, (1, LANES), 1) >= HEAD_DIM
    for h in range(N_HEADS):
        hp, sub = divmod(h, 2)
        keep = lane_hi if sub == 1 else jnp.logical_not(lane_hi)
        sl = slice(hp * LANES, (hp + 1) * LANES)
        qm_ref[h] = jnp.where(keep, q_ref[:, sl], jnp.zeros((), BF16))
        qim_ref[h] = jnp.where(keep, qi_ref[:, sl], jnp.zeros((), BF16))

    wt = wt_ref[0]
    qidx = i * tq + lax.broadcasted_iota(I32, (1, tq), 1)

    def score_body(jb, _):
        for u in range(per_big):
            off = pl.multiple_of(jb * KB + u * KC, KC)
            kc = ki_ref[pl.ds(off, KC), :]
            s = jnp.zeros((KC, tq), F32)
            for h in range(N_IDX_HEADS):
                d = _nt_dot(kc, qim_ref[h])
                s = s + wt[h:h + 1, :] * jnp.maximum(d, 0.0)
            kidx = off + lax.broadcasted_iota(I32, (KC, 1), 0)
            sc_ref[pl.ds(off, KC), :] = jnp.where(kidx <= qidx, s, NEG_INF)
        return 0

    lax.fori_loop(0, nbig, score_body, 0)

    per_sel = KS // KC
    thr, jcut = _select_threshold(sc_ref, (i + per_sel) // per_sel, topk, idx_bits)

    def madd_body(jb, _):
        off = pl.multiple_of(jb * KB, KB)
        k = sc_ref[pl.ds(off, KB), :]
        kidx = off + lax.broadcasted_iota(I32, (KB, 1), 0)
        sel = jnp.logical_or(k > thr, jnp.logical_and(k == thr, kidx <= jcut))
        madd_ref[pl.ds(off, KB), :] = jnp.where(sel, 0.0, NEG_INF).astype(F32)
        return 0

    lax.fori_loop(0, nbig, madd_body, 0)

    ones_rows = jnp.ones((ONES_ROWS, KB), BF16)
    zero_acc = jnp.zeros((HEAD_DIM + ONES_ROWS, tq), F32)

    def logits_step(h, lg_ref, jb):
        off = pl.multiple_of(jb * KB, KB)
        loff = pl.multiple_of((h // 2) * LANES, LANES)
        lg_ref[pl.ds(off, KB), :] = (_nt_dot(k_ref[pl.ds(off, KB), pl.ds(loff, LANES)], qm_ref[h]) * LOG2E
                                     + madd_ref[pl.ds(off, KB), :])

    def finish_logits(h, lg_ref):
        diag = pl.ds(pl.multiple_of(i * KC, KC), KC)
        lg_ref[diag, :] = lg_ref[diag, :] + bias_ref[0, h]

        @pl.when(i >= 1)
        def _():
            prev = pl.ds(pl.multiple_of((i - 1) * KC, KC), KC)
            lg_ref[prev, :] = lg_ref[prev, :] + bias_ref[1, h]

        def max_step(jb, mx):
            return _fold_rows(mx, lg_ref[pl.ds(pl.multiple_of(jb * KB, KB), KB), :], jnp.maximum)

        mx = lax.fori_loop(0, nbig, max_step, jnp.full((FOLD_ROWS, tq), NEG_INF, F32))
        return jnp.max(mx, axis=0, keepdims=True)

    def pv_step(h, lg_ref, m, jb, acc):
        off = pl.multiple_of(jb * KB, KB)
        p = jnp.exp2(lg_ref[pl.ds(off, KB), :] - m)
        vrow = pl.multiple_of(h * HEAD_DIM, HEAD_DIM)
        vsl = jnp.concatenate([vt_ref[0, pl.ds(vrow, HEAD_DIM), pl.ds(off, KB)], ones_rows], axis=0)
        return acc + _dot(vsl, p.astype(BF16))

    def write_out(h, acc):
        vrow = pl.multiple_of(h * HEAD_DIM, HEAD_DIM)
        o_ref[pl.ds(vrow, HEAD_DIM), :] = (acc[:HEAD_DIM] / acc[HEAD_DIM:HEAD_DIM + 1]).astype(o_ref.dtype)

    def sweep(step, init):
        def two(t, carry):
            return step(2 * t + 1, step(2 * t, carry))

        carry = lax.fori_loop(0, nbig // 2, two, init)
        return lax.cond(nbig % 2 == 1, lambda c: step(nbig - 1, c), lambda c: c, carry)

    def advance(h, lg_new, lg_old, m_old):
        def fused_step(jb, acc):
            logits_step(h, lg_new, jb)
            return pv_step(h - 1, lg_old, m_old, jb, acc)

        write_out(h - 1, sweep(fused_step, zero_acc))
        return finish_logits(h, lg_new)

    def first_logits(jb, carry):
        logits_step(0, lga_ref, jb)
        return carry

    sweep(first_logits, jnp.int32(0))

    def pair_body(t, m_even):
        m_odd = advance(2 * t + 1, lgb_ref, lga_ref, m_even)
        return advance(2 * t + 2, lga_ref, lgb_ref, m_odd)

    m_even = lax.fori_loop(0, N_HEADS // 2 - 1, pair_body, finish_logits(0, lga_ref))
    m_last = advance(N_HEADS - 1, lgb_ref, lga_ref, m_even)
    write_out(N_HEADS - 1, sweep(lambda jb, acc: pv_step(N_HEADS - 1, lgb_ref, m_last, jb, acc), zero_acc))


def _prompt_attention(qb, qib, wit, kb, ki2b, vt, bias_t, batch, seq):
    assert seq % KB == 0 and KB % KC == 0 and TQ == KC
    nq = seq // TQ
    topk = min(TOPK_MAX, seq // 4)
    idx_bits = max(1, (seq - 1).bit_length())
    n = batch * seq
    return pl.pallas_call(
        functools.partial(_prompt_attn_body, topk=topk, idx_bits=idx_bits),
        grid=(batch, nq),
        in_specs=[
            pl.BlockSpec((TQ, D_ATT), lambda b, i: (b * nq + i, 0)),
            pl.BlockSpec((TQ, N_IDX_HEADS * IDX_DIM), lambda b, i: (b * nq + i, 0)),
            pl.BlockSpec((1, N_IDX_HEADS, TQ), lambda b, i: (b, 0, i)),
            pl.BlockSpec((seq, D_ATT), lambda b, i: (b, 0)),
            pl.BlockSpec((seq, 2 * IDX_DIM), lambda b, i: (b, 0)),
            pl.BlockSpec((1, D_ATT, seq), lambda b, i: (b, 0, 0)),
            pl.BlockSpec(bias_t.shape, lambda b, i: (0, 0, 0, 0)),
        ],
        out_specs=pl.BlockSpec((D_ATT, TQ), lambda b, i: (0, b * nq + i)),
        out_shape=jax.ShapeDtypeStruct((D_ATT, n), BF16),
        scratch_shapes=[
            pltpu.VMEM((seq, TQ), F32),
            pltpu.VMEM((seq, TQ), F32),
            pltpu.VMEM((N_HEADS, TQ, LANES), BF16),
            pltpu.VMEM((N_IDX_HEADS, TQ, LANES), BF16),
            pltpu.VMEM((seq, TQ), F32),
            pltpu.VMEM((seq, TQ), F32),
        ],
        compiler_params=pltpu.CompilerParams(dimension_semantics=("arbitrary", "arbitrary"),
                                             vmem_limit_bytes=VMEM_LIMIT),
        name="prompt_attention",
    )(qb, qib, wit, kb, ki2b, vt, bias_t)


def _prompt_bias_tiles(rel_bias):
    shifted = (rel_bias - rel_bias[N_BUCKETS - 1:N_BUCKETS]).T.astype(F32)
    c = jnp.arange(KC, dtype=I32)[:, None]
    r = jnp.arange(TQ, dtype=I32)[None, :]
    tiles = []
    for base in (0, TQ):
        dist = base + r - c
        bucket = jnp.where(dist >= 0, _t5_bucket(dist), N_BUCKETS - 1)
        tiles.append(_bucket_lookup(shifted, bucket))
    return jnp.stack(tiles) * LOG2E


def _lru_gates(xc, wa_ref, ba_ref, wx_ref, bx_ref, lam_ref, precision):
    xm = xc.astype(wa_ref.dtype)
    r = jax.nn.sigmoid(_dot(xm, wa_ref[...], precision) + ba_ref[...])
    g = jax.nn.sigmoid(_dot(xm, wx_ref[...], precision) + bx_ref[...])
    log_a = -LRU_C * r * jax.nn.softplus(-lam_ref[...])
    a = jnp.exp(log_a)
    u = jnp.sqrt(-jnp.tanh(log_a) * (a * a + 1.0)) * g * xc
    return a, u


def _rglru_prompt_body(xr_ref, gr_ref, cw_ref, cb_ref, wa_ref, ba_ref, wx_ref, bx_ref, lam_ref,
                       rnn_ref, hlast_ref, ext_ref, hc_ref):
    tt = xr_ref.shape[0]

    @pl.when(pl.program_id(1) == 0)
    def _():
        ext_ref[0:SUBLANES, :] = jnp.zeros((SUBLANES, ext_ref.shape[1]), F32)
        hc_ref[...] = jnp.zeros_like(hc_ref)

    x = xr_ref[...]
    ext_ref[SUBLANES:SUBLANES + tt, :] = x
    cw = cw_ref[...]
    xc = cb_ref[...] + ext_ref[SUBLANES - 3:SUBLANES - 3 + tt, :] * cw[0:1]
    for j in range(1, CONV_W):
        s = CONV_W - 1 - j
        xc = xc + ext_ref[SUBLANES - s:SUBLANES - s + tt, :] * cw[j:j + 1]
    ext_ref[0:SUBLANES, :] = x[tt - SUBLANES:tt]

    a, u = _lru_gates(xc, wa_ref, ba_ref, wx_ref, bx_ref, lam_ref, None)
    row = lax.broadcasted_iota(I32, (tt, 1), 0)
    d = 1
    while d < tt:
        keep = row >= d
        a_sh = jnp.where(keep, pltpu.roll(a, d, 0), 1.0)
        u_sh = jnp.where(keep, pltpu.roll(u, d, 0), 0.0)
        u = u + a * u_sh
        a = a * a_sh
        d *= 2
    h = u + a * hc_ref[...]
    h_last = h[tt - 1:tt]
    hc_ref[...] = h_last
    hlast_ref[0] = h_last
    rnn_ref[...] = (h * jax.nn.gelu(gr_ref[...])).astype(rnn_ref.dtype)


def _rglru_prompt(xr, gr, cw, cb, wa_bd, ba, wx_bd, bx, lam, batch, seq, tt):
    n, dr = xr.shape
    nt = seq // tt
    full = lambda shape: pl.BlockSpec(shape, lambda b, t: (0,) * len(shape))
    return pl.pallas_call(
        _rglru_prompt_body,
        grid=(batch, nt),
        in_specs=[
            pl.BlockSpec((tt, dr), lambda b, t: (b * nt + t, 0)),
            pl.BlockSpec((tt, dr), lambda b, t: (b * nt + t, 0)),
            full(cw.shape), full(cb.shape), full(wa_bd.shape), full(ba.shape),
            full(wx_bd.shape), full(bx.shape), full(lam.shape),
        ],
        out_specs=[
            pl.BlockSpec((tt, dr), lambda b, t: (b * nt + t, 0)),
            pl.BlockSpec((1, 1, dr), lambda b, t: (b, 0, 0)),
        ],
        out_shape=[jax.ShapeDtypeStruct((n, dr), BF16), jax.ShapeDtypeStruct((batch, 1, dr), F32)],
        scratch_shapes=[pltpu.VMEM((tt + SUBLANES, dr), F32), pltpu.VMEM((1, dr), F32)],
        compiler_params=pltpu.CompilerParams(dimension_semantics=("arbitrary", "arbitrary"),
                                             vmem_limit_bytes=VMEM_LIMIT),
        name="rglru_prompt",
    )(xr, gr, cw, cb, wa_bd, ba, wx_bd, bx, lam)


def _rglru_sample_body(xr_ref, gr_ref, cp_ref, hp_ref, cw_ref, cb_ref, wa_ref, ba_ref, wx_ref, bx_ref,
                       lam_ref, rnn_ref, h_ref):
    cw = cw_ref[...]
    xc = cb_ref[...] + cp_ref[0] * cw[0:1]
    for j in range(1, CONV_W - 1):
        xc = xc + cp_ref[j] * cw[j:j + 1]
    xc = xc + xr_ref[...] * cw[CONV_W - 1:CONV_W]
    a, u = _lru_gates(xc, wa_ref, ba_ref, wx_ref, bx_ref, lam_ref, lax.Precision.HIGHEST)
    h = u + a * hp_ref[...]
    h_ref[...] = h
    rnn_ref[...] = h * jax.nn.gelu(gr_ref[...])


def _rglru_sample(xr, gr, conv_prev_t, h_prev, cw, cb, wa_bd, ba, wx_bd, bx, lam):
    n, dr = xr.shape
    return pl.pallas_call(
        _rglru_sample_body,
        out_shape=[jax.ShapeDtypeStruct((n, dr), F32), jax.ShapeDtypeStruct((n, dr), F32)],
        compiler_params=pltpu.CompilerParams(vmem_limit_bytes=VMEM_LIMIT),
        name="rglru_sample",
    )(xr, gr, conv_prev_t, h_prev, cw, cb, wa_bd, ba, wx_bd, bx, lam)


def _layer_norm(x, g, b):
    mu = jnp.mean(x, axis=-1, keepdims=True)
    xc = x - mu
    var = jnp.mean(jnp.square(xc), axis=-1, keepdims=True)
    return xc * lax.rsqrt(var + LN_EPS) * g + b


def _mlp_body(attn_ref, rnn_ref, x_ref, woa_ref, wor_ref, g1_ref, b1_ref, wup_ref, bup_ref,
              wdn_ref, bdn_ref, g2_ref, b2_ref, y_ref, *, alpha, attn_channel_major, ff_chunk):
    cdt = woa_ref.dtype
    if attn_channel_major:
        mix = lax.dot_general(attn_ref[...].astype(cdt), woa_ref[...], (((0,), (0,)), ((), ())),
                              preferred_element_type=F32)
    else:
        mix = _dot(attn_ref[...].astype(cdt), woa_ref[...])
    mix = mix + _dot(rnn_ref[...].astype(cdt), wor_ref[...])
    x1 = _layer_norm(alpha * x_ref[...] + mix, g1_ref[...], b1_ref[...])
    x1c = x1.astype(cdt)
    d_ff = wup_ref.shape[1]
    y = None
    for c in range(0, d_ff, ff_chunk):
        hid = _dot(x1c, wup_ref[:, c:c + ff_chunk]) + bup_ref[:, c:c + ff_chunk]
        hid = jnp.square(jnp.maximum(hid, 0.0)).astype(cdt)
        part = _dot(hid, wdn_ref[c:c + ff_chunk, :])
        y = part if y is None else y + part
    y_ref[...] = _layer_norm(alpha * x1 + y + bdn_ref[...], g2_ref[...], b2_ref[...])


def _out_mlp(attn, rnn, x2d, woa, wor, g1, b1, wup, bup, wdn, bdn, g2, b2, *, alpha, tm,
             attn_channel_major):
    n, dm = x2d.shape
    da = woa.shape[0]
    full = lambda a: pl.BlockSpec(a.shape, lambda i: (0,) * a.ndim)
    attn_spec = (pl.BlockSpec((da, tm), lambda i: (0, i)) if attn_channel_major
                 else pl.BlockSpec((tm, da), lambda i: (i, 0)))
    consts = (woa, wor, g1, b1, wup, bup, wdn, bdn, g2, b2)
    return pl.pallas_call(
        functools.partial(_mlp_body, alpha=alpha, attn_channel_major=attn_channel_major, ff_chunk=1024),
        grid=(n // tm,),
        in_specs=[attn_spec, pl.BlockSpec((tm, rnn.shape[1]), lambda i: (i, 0)),
                  pl.BlockSpec((tm, dm), lambda i: (i, 0))] + [full(a) for a in consts],
        out_specs=pl.BlockSpec((tm, dm), lambda i: (i, 0)),
        out_shape=jax.ShapeDtypeStruct((n, dm), F32),
        compiler_params=pltpu.CompilerParams(dimension_semantics=("arbitrary",),
                                             vmem_limit_bytes=VMEM_LIMIT),
        name="out_mlp",
    )(attn, rnn, x2d, *consts)


def _sample_keys_body(pt_ref, qi_ref, w_ref, kin_ref, cki_ref, keys_ref, knew_ref, kibuf, sem,
                      *, layer):
    b = pl.program_id(0)
    nb = pl.num_programs(0)
    n_pages = kibuf.shape[1]

    def page_copy(bb, slot, p):
        return pltpu.make_async_copy(cki_ref.at[layer, pt_ref[bb, p]], kibuf.at[slot, p], sem.at[slot])

    def issue_all(bb, slot):
        def body(p, _):
            page_copy(bb, slot, p).start()
            return 0
        lax.fori_loop(0, n_pages, body, 0)

    slot = lax.rem(b, 2)

    @pl.when(b == 0)
    def _():
        issue_all(b, slot)

    @pl.when(b + 1 < nb)
    def _():
        issue_all(b + 1, 1 - slot)

    def wait_body(p, _):
        page_copy(b, slot, p).wait()
        return 0

    lax.fori_loop(0, n_pages, wait_body, 0)

    qi16 = qi_ref[0].astype(BF16)
    w = w_ref[0]
    group = 16

    def score_body(g, _):
        for u in range(group):
            p = g * group + u
            d = _dot(qi16, kibuf[slot, p].astype(BF16))
            keys_ref[0, pl.ds(p, 1), :] = jnp.sum(w * jnp.maximum(d, 0.0), axis=0, keepdims=True)
        return 0

    lax.fori_loop(0, n_pages // group, score_body, 0)

    dn = jnp.sum(qi16.astype(F32) * kin_ref[0].astype(BF16).astype(F32), axis=1, keepdims=True)
    score_new = jnp.sum(w * jnp.maximum(dn, 0.0), axis=0, keepdims=True)
    knew_ref[0] = jnp.broadcast_to(score_new, (1, LANES))


def _sample_keys(page_table, qi_s, wi_col, ki_new, cache_ki_t, layer):
    bd, n_pages = page_table.shape
    assert n_pages % 16 == 0
    grid_spec = pltpu.PrefetchScalarGridSpec(
        num_scalar_prefetch=1,
        grid=(bd,),
        in_specs=[
            pl.BlockSpec((1, N_IDX_HEADS, IDX_DIM), lambda b, pt: (b, 0, 0)),
            pl.BlockSpec((1, N_IDX_HEADS, 1), lambda b, pt: (b, 0, 0)),
            pl.BlockSpec((1, 1, IDX_DIM), lambda b, pt: (b, 0, 0)),
            pl.BlockSpec(memory_space=pl.ANY),
        ],
        out_specs=[
            pl.BlockSpec((1, n_pages, PAGE_SIZE), lambda b, pt: (b, 0, 0)),
            pl.BlockSpec((1, 1, LANES), lambda b, pt: (b, 0, 0)),
        ],
        scratch_shapes=[
            pltpu.VMEM((2, n_pages, IDX_DIM, PAGE_SIZE), F32),
            pltpu.SemaphoreType.DMA((2,)),
        ],
    )
    return pl.pallas_call(
        functools.partial(_sample_keys_body, layer=layer),
        grid_spec=grid_spec,
        out_shape=[jax.ShapeDtypeStruct((bd, n_pages, PAGE_SIZE), F32),
                   jax.ShapeDtypeStruct((bd, 1, LANES), F32)],
        compiler_params=pltpu.CompilerParams(dimension_semantics=("arbitrary",),
                                             vmem_limit_bytes=VMEM_LIMIT),
        name="sample_keys",
    )(page_table, qi_s, wi_col, ki_new, cache_ki_t)


def _sample_mask_body(keys_ref, knew_ref, madd_ref, maddn_ref, *, topk, idx_bits):
    keys = keys_ref[...]
    key_new = knew_ref[:, :, 0:1]
    n_past = keys.shape[1] * keys.shape[2]
    kidx = (lax.broadcasted_iota(I32, keys.shape, 1) * PAGE_SIZE
            + lax.broadcasted_iota(I32, keys.shape, 2))

    def count(pred_past, pred_new):
        c = jnp.sum(jnp.where(pred_past, 1, 0).astype(I32), axis=1, keepdims=True)
        return jnp.sum(c, axis=2, keepdims=True) + jnp.where(pred_new, 1, 0).astype(I32)

    c0 = count(keys >= 0.0, key_new >= 0.0)
    nonneg = c0 >= topk
    prefix = jnp.where(nonneg, 0, INT_MIN).astype(I32)
    cge = jnp.where(nonneg, c0, n_past + 1).astype(I32)

    def bit_body(it, carry):
        prefix, cge = carry
        cand = prefix | lax.shift_left(jnp.int32(1), 30 - it)
        cand_f = _key_to_float(cand)
        c = count(keys >= cand_f, key_new >= cand_f)
        ok = c >= topk
        return jnp.where(ok, cand, prefix), jnp.where(ok, c, cge)

    thr_key, cge = lax.fori_loop(0, 31, bit_body, (prefix, cge))
    real = thr_key > NEG_INF_KEY
    thr = jnp.where(real, _key_to_float(thr_key), NEG_INF)
    need = jnp.logical_and(cge > topk, real)
    j_default = jnp.where(real, jnp.int32(2 ** 30), jnp.int32(-1))

    def tie_fn():
        want = topk - count(keys > thr, key_new > thr)

        def body(it, p):
            cand = p | lax.shift_left(jnp.int32(1), idx_bits - 1 - it)
            c = count(jnp.logical_and(keys == thr, kidx < cand),
                      jnp.logical_and(key_new == thr, n_past < cand))
            return jnp.where(c < want, cand, p)

        p = lax.fori_loop(0, idx_bits, body, jnp.zeros(thr_key.shape, I32))
        return jnp.where(need, p, j_default)

    jcut = lax.cond(jnp.max(jnp.where(need, 1, 0)) > 0, tie_fn, lambda: j_default)
    sel = jnp.logical_or(keys > thr, jnp.logical_and(keys == thr, kidx <= jcut))
    sel_new = jnp.logical_or(key_new > thr, jnp.logical_and(key_new == thr, n_past <= jcut))
    madd_ref[...] = jnp.where(sel, 0.0, NEG_INF).astype(F32)
    maddn_ref[...] = jnp.broadcast_to(jnp.where(sel_new, 0.0, NEG_INF).astype(F32), maddn_ref.shape)


def _sample_mask(keys, key_new, topk):
    bd, n_pages, _ = keys.shape
    idx_bits = max(1, (n_pages * PAGE_SIZE).bit_length())
    return pl.pallas_call(
        functools.partial(_sample_mask_body, topk=topk, idx_bits=idx_bits),
        out_shape=[jax.ShapeDtypeStruct(keys.shape, F32), jax.ShapeDtypeStruct(key_new.shape, F32)],
        compiler_params=pltpu.CompilerParams(vmem_limit_bytes=VMEM_LIMIT),
        name="sample_mask",
    )(keys, key_new)


def _sample_attend_body(pt_ref, qb_ref, q_ref, kn_ref, vn_ref, madd_ref, maddn_ref, btab_ref,
                        ck_ref, cv_ref, o_ref, ring_ref, lg_ref, acc_ref, sem, *, layer):
    b = pl.program_id(0)
    n_pages = madd_ref.shape[1]
    depth = ring_ref.shape[0]
    group = 4
    per_seq = 2 * n_pages
    total_items = pl.num_programs(0) * per_seq
    base = b * per_seq

    def item_copy(t, cache_ref, seq_i, page_i):
        slot = lax.rem(t, depth)
        return pltpu.make_async_copy(cache_ref.at[layer, pt_ref[seq_i, page_i]], ring_ref.at[slot],
                                     sem.at[slot])

    def start_item(t):
        seq_i = lax.div(t, per_seq)
        r = t - seq_i * per_seq
        is_v = r >= n_pages

        @pl.when(jnp.logical_not(is_v))
        def _():
            item_copy(t, ck_ref, seq_i, r).start()

        @pl.when(is_v)
        def _():
            item_copy(t, cv_ref, seq_i, r - n_pages).start()

    def finish_item(t):
        item_copy(t, ck_ref, 0, 0).wait()

    def refill(t):
        @pl.when(t + depth < total_items)
        def _():
            start_item(t + depth)

    @pl.when(b == 0)
    def _():
        def prime(t, _):
            start_item(t)
            return 0
        lax.fori_loop(0, depth, prime, 0)

    def k_body(g, _):
        for u in range(group):
            p = g * group + u
            t = base + p
            finish_item(t)
            slot = lax.rem(t, depth)
            mrow = madd_ref[0, pl.ds(p, 1), :]
            is_last = p == n_pages - 1
            for h in range(N_HEADS):
                lg = jnp.sum(ring_ref[slot, h] * qb_ref[0, h], axis=0, keepdims=True) + mrow
                lg_ref[h, pl.ds(p, 1), :] = lg + jnp.where(is_last, btab_ref[h:h + 1, 0:PAGE_SIZE], 0.0)
            refill(t)
        return 0

    lax.fori_loop(0, n_pages // group, k_body, 0)

    def total(x, op):
        return op(op(x, axis=0, keepdims=True), axis=1, keepdims=True)

    lg_new = (jnp.sum(q_ref[0] * kn_ref[0], axis=1, keepdims=True)
              + btab_ref[:, PAGE_SIZE:PAGE_SIZE + 1] + maddn_ref[0][:, 0:1])
    p_new, l = [], []
    for h in range(N_HEADS):
        lgh = lg_ref[h]
        mh = jnp.maximum(total(lgh, jnp.max), lg_new[h:h + 1])
        prh = jnp.exp(lgh - mh)
        lg_ref[h] = prh
        p_new.append(jnp.exp(lg_new[h:h + 1] - mh))
        l.append(total(prh, jnp.sum) + p_new[h])

    acc_ref[...] = jnp.zeros_like(acc_ref)

    def v_body(g, _):
        parts = [None] * N_HEADS
        for u in range(group):
            p = g * group + u
            t = base + n_pages + p
            finish_item(t)
            slot = lax.rem(t, depth)
            for h in range(N_HEADS):
                term = ring_ref[slot, h] * lg_ref[h, pl.ds(p, 1), :]
                parts[h] = term if parts[h] is None else parts[h] + term
            refill(t)
        for h in range(N_HEADS):
            acc_ref[h] += parts[h]
        return 0

    lax.fori_loop(0, n_pages // group, v_body, 0)

    for h in range(N_HEADS):
        past = jnp.sum(acc_ref[h].T, axis=0, keepdims=True)
        o_ref[0, h:h + 1, :] = (past + p_new[h] * vn_ref[0, h:h + 1, :]) / l[h]


def _sample_attend(page_table, q_lanes, q_s, k_new, v_new, madd, madd_new, btab, cache_k_t, cache_v_t,
                   layer):
    bd, n_pages = page_table.shape
    assert n_pages % 4 == 0 and 2 * n_pages >= PAGE_RING
    blk3 = lambda b, pt: (b, 0, 0)
    grid_spec = pltpu.PrefetchScalarGridSpec(
        num_scalar_prefetch=1,
        grid=(bd,),
        in_specs=[
            pl.BlockSpec((1, N_HEADS, HEAD_DIM, PAGE_SIZE), lambda b, pt: (b, 0, 0, 0)),
            pl.BlockSpec((1, N_HEADS, HEAD_DIM), blk3),
            pl.BlockSpec((1, N_HEADS, HEAD_DIM), blk3),
            pl.BlockSpec((1, N_HEADS, HEAD_DIM), blk3),
            pl.BlockSpec((1, n_pages, PAGE_SIZE), blk3),
            pl.BlockSpec((1, 1, LANES), blk3),
            pl.BlockSpec(btab.shape, lambda b, pt: (0, 0)),
            pl.BlockSpec(memory_space=pl.ANY),
            pl.BlockSpec(memory_space=pl.ANY),
        ],
        out_specs=pl.BlockSpec((1, N_HEADS, HEAD_DIM), blk3),
        scratch_shapes=[
            pltpu.VMEM((PAGE_RING, N_HEADS, HEAD_DIM, PAGE_SIZE), F32),
            pltpu.VMEM((N_HEADS, n_pages, PAGE_SIZE), F32),
            pltpu.VMEM((N_HEADS, HEAD_DIM, PAGE_SIZE), F32),
            pltpu.SemaphoreType.DMA((PAGE_RING,)),
        ],
    )
    return pl.pallas_call(
        functools.partial(_sample_attend_body, layer=layer),
        grid_spec=grid_spec,
        out_shape=jax.ShapeDtypeStruct((bd, N_HEADS, HEAD_DIM), F32),
        compiler_params=pltpu.CompilerParams(dimension_semantics=("arbitrary",),
                                             vmem_limit_bytes=VMEM_LIMIT),
        name="sample_attend",
    )(page_table, q_lanes, q_s, k_new, v_new, madd, madd_new, btab, cache_k_t, cache_v_t)


def _block_diag(w):
    nb, c, d = w.shape
    eye = jnp.eye(nb, dtype=w.dtype)
    return (eye[:, None, :, None] * w[:, :, None, :]).reshape(nb * c, nb * d)


def kernel(x_prompt, x_sample, cache_k, cache_v, cache_k_idx, state_h, state_conv, page_table,
           rel_bias, w_in, conv_w, conv_b, w_a, b_a, w_x, b_x, lru_lambda, w_out,
           ln1_g, ln1_b, w_up, b_up, w_down, b_down, ln2_g, ln2_b):
    depth = w_in.shape[0]
    batch, seq, dm = x_prompt.shape
    bd, dec_seq, _ = x_sample.shape
    assert dec_seq == 1, "the decode pass handles one new token per sequence"
    d_rnn = dm - D_ATT
    alpha = (2 * depth) ** 0.25
    n_pages = page_table.shape[1]
    n_past = n_pages * PAGE_SIZE
    topk_s = min(TOPK_MAX, (n_past + dec_seq) // 4)
    row = lambda a: a.reshape(1, -1)
    c16 = lambda a: a.astype(BF16)

    sizes = (D_ATT, D_ATT, D_ATT, N_IDX_HEADS * IDX_DIM, IDX_DIM, N_IDX_HEADS, d_rnn, d_rnn)
    offs = [0]
    for s in sizes:
        offs.append(offs[-1] + s)

    w_in_t = jnp.swapaxes(w_in, 1, 2)
    cache_ki_t = jnp.swapaxes(cache_k_idx, 2, 3)
    cache_k_t = jnp.transpose(cache_k, (0, 1, 3, 4, 2))
    cache_v_t = jnp.transpose(cache_v, (0, 1, 3, 4, 2))

    bias_t = _prompt_bias_tiles(rel_bias)
    shifted = (rel_bias - rel_bias[N_BUCKETS - 1:N_BUCKETS]).T.astype(F32)
    btab = jnp.concatenate([
        _bucket_lookup(shifted, _t5_bucket(PAGE_SIZE - jnp.arange(PAGE_SIZE, dtype=I32))),
        _bucket_lookup(shifted, _t5_bucket(jnp.zeros((PAGE_SIZE,), I32)))], axis=1)

    yp = x_prompt.reshape(batch * seq, dm)
    ys = x_sample.reshape(bd, dm)
    outs = [[] for _ in range(10)]
    for layer in range(depth):
        wt = w_in_t[layer]
        wq, wk, wv, wqi, wki, wwi, wxr, wgr = [wt[offs[j]:offs[j + 1]] for j in range(8)]
        cw, cb, lam = conv_w[layer], row(conv_b[layer]), row(lru_lambda[layer])
        wa_bd, wx_bd = _block_diag(w_a[layer]), _block_diag(w_x[layer])
        ba, bx = row(b_a[layer]), row(b_x[layer])
        woa, wor = c16(w_out[layer][:D_ATT]), c16(w_out[layer][D_ATT:])
        mlp_consts = (woa, wor, row(ln1_g[layer]), row(ln1_b[layer]), c16(w_up[layer]), row(b_up[layer]),
                      c16(w_down[layer]), row(b_down[layer]), row(ln2_g[layer]), row(ln2_b[layer]))

        p_weights = [c16(wq), c16(wk), c16(wk), c16(wv), c16(wqi), c16(jnp.concatenate([wki, wki], 0)),
                     c16(wki), c16(wwi), c16(wxr), c16(wgr)]
        p_plan = (
            (False, ((0, D_ATT, HEAD_DIM ** -0.5, BF16),)),
            (False, ((0, D_ATT, 1.0, BF16),)),
            (True, ((0, D_ATT, 1.0, F32),)),
            (True, ((0, D_ATT, 1.0, F32), (0, D_ATT, 1.0, BF16))),
            (False, ((0, N_IDX_HEADS * IDX_DIM, 1.0, BF16),)),
            (False, ((0, 2 * IDX_DIM, 1.0, BF16),)),
            (True, ((0, IDX_DIM, 1.0, F32),)),
            (True, ((0, N_IDX_HEADS, 1.0, F32),)),
            (False, ((0, d_rnn, 1.0, F32),)),
            (False, ((0, d_rnn, 1.0, F32),)),
        )
        qb, kb, k_t, v_t, vtb, qib, ki2b, ki_t, wit, xr_p, gr_p = _project(
            yp, p_weights, p_plan, batch, seq, tm=512)
        attn_t = _prompt_attention(qb, qib, wit, kb, ki2b, vtb, bias_t, batch, seq)
        rnn_p, h_p = _rglru_prompt(xr_p, gr_p, cw, cb, c16(wa_bd), ba, c16(wx_bd), bx, lam,
                                   batch, seq, tt=512)
        conv_p = xr_p.reshape(batch, seq, d_rnn)[:, seq - (CONV_W - 1):]
        yp = _out_mlp(attn_t, rnn_p, yp, *mlp_consts, alpha=alpha, tm=512, attn_channel_major=True)

        pad_wi = jnp.pad(wwi, ((0, LANES - N_IDX_HEADS), (0, 0)))
        s_weights = [wq, wk, wv, wqi, wki, pad_wi, wxr, wgr]
        s_plan = (
            (False, ((0, D_ATT, HEAD_DIM ** -0.5, F32),)),
            (False, ((0, D_ATT, 1.0, F32),)),
            (False, ((0, D_ATT, 1.0, F32),)),
            (False, ((0, N_IDX_HEADS * IDX_DIM, 1.0, F32),)),
            (False, ((0, IDX_DIM, 1.0, F32),)),
            (False, ((0, LANES, 1.0, F32),)),
            (False, ((0, d_rnn, 1.0, F32),)),
            (False, ((0, d_rnn, 1.0, F32),)),
        )
        q_s, k_s, v_s, qi_s, ki_s, wi_s, xr_s, gr_s = _project(ys, s_weights, s_plan, 1, bd, tm=bd,
                                                               precision=lax.Precision.HIGHEST)
        keys, key_new = _sample_keys(page_table, qi_s.reshape(bd, N_IDX_HEADS, IDX_DIM),
                                     wi_s[:, :N_IDX_HEADS].reshape(bd, N_IDX_HEADS, 1),
                                     ki_s.reshape(bd, 1, IDX_DIM), cache_ki_t, layer)
        madd, madd_new = _sample_mask(keys, key_new, topk_s)
        heads = lambda a: a.reshape(bd, N_HEADS, HEAD_DIM)
        q_lanes = jnp.broadcast_to(heads(q_s)[..., None], (bd, N_HEADS, HEAD_DIM, PAGE_SIZE))
        attn_s = _sample_attend(page_table, q_lanes, heads(q_s), heads(k_s), heads(v_s), madd, madd_new,
                                btab, cache_k_t, cache_v_t, layer)
        conv_prev = state_conv[layer]
        rnn_s, h_s = _rglru_sample(xr_s, gr_s, jnp.swapaxes(conv_prev, 0, 1), state_h[layer], cw, cb,
                                   wa_bd, ba, wx_bd, bx, lam)
        conv_s = jnp.concatenate([conv_prev[:, 1:], xr_s[:, None, :]], axis=1)
        ys = _out_mlp(attn_s.reshape(bd, D_ATT), rnn_s, ys, *mlp_consts, alpha=alpha, tm=bd,
                      attn_channel_major=False)

        per_head_t = lambda a: jnp.transpose(a.reshape(batch, N_HEADS, HEAD_DIM, seq), (0, 3, 1, 2))
        layer_outs = (
            per_head_t(k_t), per_head_t(v_t), jnp.swapaxes(ki_t, 1, 2), h_p.reshape(batch, d_rnn), conv_p,
            k_s.reshape(bd, dec_seq, N_HEADS, HEAD_DIM), v_s.reshape(bd, dec_seq, N_HEADS, HEAD_DIM),
            ki_s.reshape(bd, dec_seq, IDX_DIM), h_s, conv_s,
        )
        for lst, a in zip(outs, layer_outs):
            lst.append(a)

    stacked = [jnp.stack(lst) for lst in outs]
    return (yp.reshape(batch, seq, dm), ys.reshape(bd, dec_seq, dm), *stacked)
```

```python
import functools
import math

import jax
import jax.numpy as jnp
from jax import lax
from jax.experimental import pallas as pl
from jax.experimental.pallas import tpu as pltpu

F32, BF16, I32 = jnp.float32, jnp.bfloat16, jnp.int32

N_HEADS = 8
HEAD_DIM = 64
D_ATT = N_HEADS * HEAD_DIM
N_IDX_HEADS = 8
IDX_DIM = 64
TOPK_MAX = 256
N_RNN_BLOCKS = 8
CONV_W = 4
LRU_C = 8.0
N_BUCKETS = 32
MAX_DISTANCE = 128
LN_EPS = 1e-5
PAGE_SIZE = 128

LANES = 128
SUBLANES = 8
VMEM_LIMIT = 56 * 1024 * 1024

INT_MIN = -(2 ** 31)
NEG_INF = float("-inf")

TQ = 256
KC = 256
KB = 1024
KS = 512
ONES_ROWS = 16
LOG2E = 1.4426950408889634
PAGE_RING = 32


def _nt_dot(a, b, precision=None):
    return lax.dot_general(a, b, (((1,), (1,)), ((), ())), precision=precision,
                           preferred_element_type=F32)


def _dot(a, b, precision=None):
    return jnp.dot(a, b, precision=precision, preferred_element_type=F32)


def _t5_bucket(dist):
    dist = jnp.maximum(dist, 0)
    max_exact = N_BUCKETS // 2
    d = jnp.maximum(dist, 1).astype(F32)
    scaled = jnp.log(d / max_exact) / math.log(MAX_DISTANCE / max_exact) * (N_BUCKETS - max_exact)
    large = max_exact + jnp.floor(jnp.maximum(scaled, 0.0)).astype(I32)
    large = jnp.minimum(large, N_BUCKETS - 1)
    return jnp.where(dist < max_exact, dist, large)


def _bucket_lookup(table, bucket):
    out = jnp.zeros((table.shape[0],) + bucket.shape, F32)
    for b in range(table.shape[1]):
        out = jnp.where(bucket[None] == b, table[:, b].reshape((-1,) + (1,) * bucket.ndim), out)
    return out


def _key_to_float(key):
    bits = jnp.where(key >= 0, key, jnp.int32(INT_MIN) - key)
    return lax.bitcast_convert_type(bits, F32)


NEG_INF_KEY = INT_MIN + 0x00800000


def _proj_body(x_ref, *refs, plan, precision):
    n_w = len(plan)
    w_refs, o_refs = refs[:n_w], refs[n_w:]
    x = x_ref[...].astype(w_refs[0].dtype)
    oi = 0
    for (channel_major, outs), w_ref in zip(plan, w_refs):
        y = _nt_dot(w_ref[...], x, precision) if channel_major else _nt_dot(x, w_ref[...], precision)
        for lo, hi, scale, dtype in outs:
            part = y[lo:hi, :] if channel_major else y[:, lo:hi]
            if scale != 1.0:
                part = part * scale
            if channel_major:
                o_refs[oi][0] = part.astype(dtype)
            else:
                o_refs[oi][...] = part.astype(dtype)
            oi += 1


def _project(x2d, weights_t, plan, batch, seq, tm, precision=None):
    n, d = x2d.shape
    assert n == batch * seq and seq % tm == 0
    nt = seq // tm
    in_specs = [pl.BlockSpec((tm, d), lambda b, t: (b * nt + t, 0))]
    for w in weights_t:
        in_specs.append(pl.BlockSpec(w.shape, lambda b, t: (0, 0)))
    out_shapes, out_specs = [], []
    for (channel_major, outs) in plan:
        for lo, hi, _, dtype in outs:
            if channel_major:
                out_shapes.append(jax.ShapeDtypeStruct((batch, hi - lo, seq), dtype))
                out_specs.append(pl.BlockSpec((1, hi - lo, tm), lambda b, t: (b, 0, t)))
            else:
                out_shapes.append(jax.ShapeDtypeStruct((n, hi - lo), dtype))
                out_specs.append(pl.BlockSpec((tm, hi - lo), lambda b, t: (b * nt + t, 0)))
    return pl.pallas_call(
        functools.partial(_proj_body, plan=plan, precision=precision),
        grid=(batch, nt),
        in_specs=in_specs,
        out_specs=out_specs,
        out_shape=out_shapes,
        compiler_params=pltpu.CompilerParams(dimension_semantics=("arbitrary", "arbitrary"),
                                             vmem_limit_bytes=VMEM_LIMIT),
        name="in_proj",
    )(x2d, *weights_t)


FOLD_ROWS = 8 * SUBLANES


def _fold_rows(acc, x, op):
    for r in range(0, x.shape[0], FOLD_ROWS):
        acc = op(acc, x[r:r + FOLD_ROWS])
    return acc


def _count_rows(keys_ref, nsteps, pred):
    tq = keys_ref.shape[1]

    def body(j, acc):
        off = pl.multiple_of(j * KS, KS)
        m = pred(keys_ref[pl.ds(off, KS), :], off)
        return _fold_rows(acc, jnp.where(m, 1, 0).astype(I32), jnp.add)

    acc = lax.fori_loop(0, nsteps, body, jnp.zeros((FOLD_ROWS, tq), I32))
    return jnp.sum(acc, axis=0, keepdims=True)


def _select_threshold(sc_ref, nbig, topk, idx_bits):
    tq = sc_ref.shape[1]
    c0 = _count_rows(sc_ref, nbig, lambda s, off: s >= 0.0)
    nonneg = c0 >= topk
    prefix = jnp.where(nonneg, 0, INT_MIN).astype(I32)
    cge = jnp.where(nonneg, c0, nbig * KS).astype(I32)

    def bit_body(it, carry):
        prefix, cge = carry
        cand = prefix | lax.shift_left(jnp.int32(1), 30 - it)
        cand_f = _key_to_float(cand)
        c = _count_rows(sc_ref, nbig, lambda s, off: s >= cand_f)
        ok = c >= topk
        return jnp.where(ok, cand, prefix), jnp.where(ok, c, cge)

    thr_key, cge = lax.fori_loop(0, 31, bit_body, (prefix, cge))
    real = thr_key > NEG_INF_KEY
    thr = jnp.where(real, _key_to_float(thr_key), NEG_INF)
    need = jnp.logical_and(cge > topk, real)
    j_default = jnp.where(real, jnp.int32(2 ** 30), jnp.int32(-1))

    def tie_fn():
        cgt = _count_rows(sc_ref, nbig, lambda s, off: s > thr)
        want = topk - cgt

        def body(it, p):
            cand = p | lax.shift_left(jnp.int32(1), idx_bits - 1 - it)

            def pred(s, off):
                kidx = off + lax.broadcasted_iota(I32, (KS, 1), 0)
                return jnp.logical_and(s == thr, kidx < cand)

            c = _count_rows(sc_ref, nbig, pred)
            return jnp.where(c < want, cand, p)

        p = lax.fori_loop(0, idx_bits, body, jnp.zeros((1, tq), I32))
        return jnp.where(need, p, j_default)

    any_need = jnp.max(jnp.where(need, 1, 0)) > 0
    jcut = lax.cond(any_need, tie_fn, lambda: j_default)
    return thr, jcut


def _prompt_attn_body(q_ref, qi_ref, wt_ref, k_ref, ki_ref, vt_ref, bias_ref, o_ref,
                      sc_ref, madd_ref, qm_ref, qim_ref, lga_ref, lgb_ref, *, topk, idx_bits):
    i = pl.program_id(1)
    tq = q_ref.shape[0]
    per_big = KB // KC
    nbig = (i + per_big) // per_big

    lane_hi = lax.broadcasted_iota(I32, (1, LANES), 1) >= HEAD_DIM
    for h in range(N_HEADS):
        hp, sub = divmod(h, 2)
        keep = lane_hi if sub == 1 else jnp.logical_not(lane_hi)
        sl = slice(hp * LANES, (hp + 1) * LANES)
        qm_ref[h] = jnp.where(keep, q_ref[:, sl], jnp.zeros((), BF16))
        qim_ref[h] = jnp.where(keep, qi_ref[:, sl], jnp.zeros((), BF16))

    wt = wt_ref[0]
    qidx = i * tq + lax.broadcasted_iota(I32, (1, tq), 1)

    def score_body(jb, _):
        for u in range(per_big):
            off = pl.multiple_of(jb * KB + u * KC, KC)
            kc = ki_ref[pl.ds(off, KC), :]
            s = jnp.zeros((KC, tq), F32)
            for h in range(N_IDX_HEADS):
                d = _nt_dot(kc, qim_ref[h])
                s = s + wt[h:h + 1, :] * jnp.maximum(d, 0.0)
            kidx = off + lax.broadcasted_iota(I32, (KC, 1), 0)
            sc_ref[pl.ds(off, KC), :] = jnp.where(kidx <= qidx, s, NEG_INF)
        return 0

    lax.fori_loop(0, nbig, score_body, 0)

    per_sel = KS // KC
    thr, jcut = _select_threshold(sc_ref, (i + per_sel) // per_sel, topk, idx_bits)

    def mask_step(jb):
        off = pl.multiple_of(jb * KB, KB)
        k = sc_ref[pl.ds(off, KB), :]
        kidx = off + lax.broadcasted_iota(I32, (KB, 1), 0)
        sel = jnp.logical_or(k > thr, jnp.logical_and(k == thr, kidx <= jcut))
        madd_ref[pl.ds(off, KB), :] = jnp.where(sel, 0.0, NEG_INF).astype(F32)

    ones_rows = jnp.ones((ONES_ROWS, KB), BF16)
    zero_acc = jnp.zeros((HEAD_DIM + ONES_ROWS, tq), F32)

    def logits_step(h, lg_ref, jb):
        off = pl.multiple_of(jb * KB, KB)
        loff = pl.multiple_of((h // 2) * LANES, LANES)
        lg_ref[pl.ds(off, KB), :] = (_nt_dot(k_ref[pl.ds(off, KB), pl.ds(loff, LANES)], qm_ref[h]) * LOG2E
                                     + madd_ref[pl.ds(off, KB), :])

    def finish_logits(h, lg_ref):
        diag = pl.ds(pl.multiple_of(i * KC, KC), KC)
        lg_ref[diag, :] = lg_ref[diag, :] + bias_ref[0, h]

        @pl.when(i >= 1)
        def _():
            prev = pl.ds(pl.multiple_of((i - 1) * KC, KC), KC)
            lg_ref[prev, :] = lg_ref[prev, :] + bias_ref[1, h]

        def max_step(jb, mx):
            return _fold_rows(mx, lg_ref[pl.ds(pl.multiple_of(jb * KB, KB), KB), :], jnp.maximum)

        mx = lax.fori_loop(0, nbig, max_step, jnp.full((FOLD_ROWS, tq), NEG_INF, F32))
        return jnp.max(mx, axis=0, keepdims=True)

    def pv_step(h, lg_ref, m, jb, acc):
        off = pl.multiple_of(jb * KB, KB)
        p = jnp.exp2(lg_ref[pl.ds(off, KB), :] - m)
        vrow = pl.multiple_of(h * HEAD_DIM, HEAD_DIM)
        vsl = jnp.concatenate([vt_ref[0, pl.ds(vrow, HEAD_DIM), pl.ds(off, KB)], ones_rows], axis=0)
        return acc + _dot(vsl, p.astype(BF16))

    def write_out(h, acc):
        vrow = pl.multiple_of(h * HEAD_DIM, HEAD_DIM)
        o_ref[pl.ds(vrow, HEAD_DIM), :] = (acc[:HEAD_DIM] / acc[HEAD_DIM:HEAD_DIM + 1]).astype(o_ref.dtype)

    def sweep(step, init):
        def two(t, carry):
            return step(2 * t + 1, step(2 * t, carry))

        carry = lax.fori_loop(0, nbig // 2, two, init)
        return lax.cond(nbig % 2 == 1, lambda c: step(nbig - 1, c), lambda c: c, carry)

    def advance(h, lg_new, lg_old, m_old):
        def fused_step(jb, acc):
            logits_step(h, lg_new, jb)
            return pv_step(h - 1, lg_old, m_old, jb, acc)

        write_out(h - 1, sweep(fused_step, zero_acc))
        return finish_logits(h, lg_new)

    def first_logits(jb, carry):
        mask_step(jb)
        logits_step(0, lga_ref, jb)
        return carry

    sweep(first_logits, jnp.int32(0))

    def pair_body(t, m_even):
        m_odd = advance(2 * t + 1, lgb_ref, lga_ref, m_even)
        return advance(2 * t + 2, lga_ref, lgb_ref, m_odd)

    m_even = lax.fori_loop(0, N_HEADS // 2 - 1, pair_body, finish_logits(0, lga_ref))
    m_last = advance(N_HEADS - 1, lgb_ref, lga_ref, m_even)
    write_out(N_HEADS - 1, sweep(lambda jb, acc: pv_step(N_HEADS - 1, lgb_ref, m_last, jb, acc), zero_acc))


def _prompt_attention(qb, qib, wit, kb, ki2b, vt, bias_t, batch, seq):
    assert seq % KB == 0 and KB % KC == 0 and TQ == KC
    nq = seq // TQ
    topk = min(TOPK_MAX, seq // 4)
    idx_bits = max(1, (seq - 1).bit_length())
    n = batch * seq
    return pl.pallas_call(
        functools.partial(_prompt_attn_body, topk=topk, idx_bits=idx_bits),
        grid=(batch, nq),
        in_specs=[
            pl.BlockSpec((TQ, D_ATT), lambda b, i: (b * nq + i, 0)),
            pl.BlockSpec((TQ, N_IDX_HEADS * IDX_DIM), lambda b, i: (b * nq + i, 0)),
            pl.BlockSpec((1, N_IDX_HEADS, TQ), lambda b, i: (b, 0, i)),
            pl.BlockSpec((seq, D_ATT), lambda b, i: (b, 0)),
            pl.BlockSpec((seq, 2 * IDX_DIM), lambda b, i: (b, 0)),
            pl.BlockSpec((1, D_ATT, seq), lambda b, i: (b, 0, 0)),
            pl.BlockSpec(bias_t.shape, lambda b, i: (0, 0, 0, 0)),
        ],
        out_specs=pl.BlockSpec((D_ATT, TQ), lambda b, i: (0, b * nq + i)),
        out_shape=jax.ShapeDtypeStruct((D_ATT, n), BF16),
        scratch_shapes=[
            pltpu.VMEM((seq, TQ), F32),
            pltpu.VMEM((seq, TQ), F32),
            pltpu.VMEM((N_HEADS, TQ, LANES), BF16),
            pltpu.VMEM((N_IDX_HEADS, TQ, LANES), BF16),
            pltpu.VMEM((seq, TQ), F32),
            pltpu.VMEM((seq, TQ), F32),
        ],
        compiler_params=pltpu.CompilerParams(dimension_semantics=("arbitrary", "arbitrary"),
                                             vmem_limit_bytes=VMEM_LIMIT),
        name="prompt_attention",
    )(qb, qib, wit, kb, ki2b, vt, bias_t)


def _prompt_bias_tiles(rel_bias):
    shifted = (rel_bias - rel_bias[N_BUCKETS - 1:N_BUCKETS]).T.astype(F32)
    c = jnp.arange(KC, dtype=I32)[:, None]
    r = jnp.arange(TQ, dtype=I32)[None, :]
    tiles = []
    for base in (0, TQ):
        dist = base + r - c
        bucket = jnp.where(dist >= 0, _t5_bucket(dist), N_BUCKETS - 1)
        tiles.append(_bucket_lookup(shifted, bucket))
    return jnp.stack(tiles) * LOG2E


def _lru_gates(xc, wa_ref, ba_ref, wx_ref, bx_ref, lam_ref, precision):
    xm = xc.astype(wa_ref.dtype)
    r = jax.nn.sigmoid(_dot(xm, wa_ref[...], precision) + ba_ref[...])
    g = jax.nn.sigmoid(_dot(xm, wx_ref[...], precision) + bx_ref[...])
    log_a = -LRU_C * r * jax.nn.softplus(-lam_ref[...])
    a = jnp.exp(log_a)
    u = jnp.sqrt(-jnp.tanh(log_a) * (a * a + 1.0)) * g * xc
    return a, u


def _rglru_prompt_body(xr_ref, gr_ref, cw_ref, cb_ref, wa_ref, ba_ref, wx_ref, bx_ref, lam_ref,
                       rnn_ref, hlast_ref, ext_ref, hc_ref):
    tt = xr_ref.shape[0]

    @pl.when(pl.program_id(1) == 0)
    def _():
        ext_ref[0:SUBLANES, :] = jnp.zeros((SUBLANES, ext_ref.shape[1]), F32)
        hc_ref[...] = jnp.zeros_like(hc_ref)

    x = xr_ref[...]
    ext_ref[SUBLANES:SUBLANES + tt, :] = x
    cw = cw_ref[...]
    xc = cb_ref[...] + ext_ref[SUBLANES - 3:SUBLANES - 3 + tt, :] * cw[0:1]
    for j in range(1, CONV_W):
        s = CONV_W - 1 - j
        xc = xc + ext_ref[SUBLANES - s:SUBLANES - s + tt, :] * cw[j:j + 1]
    ext_ref[0:SUBLANES, :] = x[tt - SUBLANES:tt]

    a, u = _lru_gates(xc, wa_ref, ba_ref, wx_ref, bx_ref, lam_ref, None)
    row = lax.broadcasted_iota(I32, (tt, 1), 0)
    d = 1
    while d < tt:
        keep = row >= d
        a_sh = jnp.where(keep, pltpu.roll(a, d, 0), 1.0)
        u_sh = jnp.where(keep, pltpu.roll(u, d, 0), 0.0)
        u = u + a * u_sh
        a = a * a_sh
        d *= 2
    h = u + a * hc_ref[...]
    h_last = h[tt - 1:tt]
    hc_ref[...] = h_last
    hlast_ref[0] = h_last
    rnn_ref[...] = (h * jax.nn.gelu(gr_ref[...])).astype(rnn_ref.dtype)


def _rglru_prompt(xr, gr, cw, cb, wa_bd, ba, wx_bd, bx, lam, batch, seq, tt):
    n, dr = xr.shape
    nt = seq // tt
    full = lambda shape: pl.BlockSpec(shape, lambda b, t: (0,) * len(shape))
    return pl.pallas_call(
        _rglru_prompt_body,
        grid=(batch, nt),
        in_specs=[
            pl.BlockSpec((tt, dr), lambda b, t: (b * nt + t, 0)),
            pl.BlockSpec((tt, dr), lambda b, t: (b * nt + t, 0)),
            full(cw.shape), full(cb.shape), full(wa_bd.shape), full(ba.shape),
            full(wx_bd.shape), full(bx.shape), full(lam.shape),
        ],
        out_specs=[
            pl.BlockSpec((tt, dr), lambda b, t: (b * nt + t, 0)),
            pl.BlockSpec((1, 1, dr), lambda b, t: (b, 0, 0)),
        ],
        out_shape=[jax.ShapeDtypeStruct((n, dr), BF16), jax.ShapeDtypeStruct((batch, 1, dr), F32)],
        scratch_shapes=[pltpu.VMEM((tt + SUBLANES, dr), F32), pltpu.VMEM((1, dr), F32)],
        compiler_params=pltpu.CompilerParams(dimension_semantics=("arbitrary", "arbitrary"),
                                             vmem_limit_bytes=VMEM_LIMIT),
        name="rglru_prompt",
    )(xr, gr, cw, cb, wa_bd, ba, wx_bd, bx, lam)


def _rglru_sample_body(xr_ref, gr_ref, cp_ref, hp_ref, cw_ref, cb_ref, wa_ref, ba_ref, wx_ref, bx_ref,
                       lam_ref, rnn_ref, h_ref):
    cw = cw_ref[...]
    xc = cb_ref[...] + cp_ref[0] * cw[0:1]
    for j in range(1, CONV_W - 1):
        xc = xc + cp_ref[j] * cw[j:j + 1]
    xc = xc + xr_ref[...] * cw[CONV_W - 1:CONV_W]
    a, u = _lru_gates(xc, wa_ref, ba_ref, wx_ref, bx_ref, lam_ref, lax.Precision.HIGHEST)
    h = u + a * hp_ref[...]
    h_ref[...] = h
    rnn_ref[...] = h * jax.nn.gelu(gr_ref[...])


def _rglru_sample(xr, gr, conv_prev_t, h_prev, cw, cb, wa_bd, ba, wx_bd, bx, lam):
    n, dr = xr.shape
    return pl.pallas_call(
        _rglru_sample_body,
        out_shape=[jax.ShapeDtypeStruct((n, dr), F32), jax.ShapeDtypeStruct((n, dr), F32)],
        compiler_params=pltpu.CompilerParams(vmem_limit_bytes=VMEM_LIMIT),
        name="rglru_sample",
    )(xr, gr, conv_prev_t, h_prev, cw, cb, wa_bd, ba, wx_bd, bx, lam)


def _layer_norm(x, g, b):
    mu = jnp.mean(x, axis=-1, keepdims=True)
    xc = x - mu
    var = jnp.mean(jnp.square(xc), axis=-1, keepdims=True)
    return xc * lax.rsqrt(var + LN_EPS) * g + b


def _mlp_body(attn_ref, rnn_ref, x_ref, woa_ref, wor_ref, g1_ref, b1_ref, wup_ref, bup_ref,
              wdn_ref, bdn_ref, g2_ref, b2_ref, y_ref, *, alpha, attn_channel_major, ff_chunk):
    cdt = woa_ref.dtype
    if attn_channel_major:
        mix = lax.dot_general(attn_ref[...].astype(cdt), woa_ref[...], (((0,), (0,)), ((), ())),
                              preferred_element_type=F32)
    else:
        mix = _dot(attn_ref[...].astype(cdt), woa_ref[...])
    mix = mix + _dot(rnn_ref[...].astype(cdt), wor_ref[...])
    x1 = _layer_norm(alpha * x_ref[...] + mix, g1_ref[...], b1_ref[...])
    x1c = x1.astype(cdt)
    d_ff = wup_ref.shape[1]
    y = None
    for c in range(0, d_ff, ff_chunk):
        hid = _dot(x1c, wup_ref[:, c:c + ff_chunk]) + bup_ref[:, c:c + ff_chunk]
        hid = jnp.square(jnp.maximum(hid, 0.0)).astype(cdt)
        part = _dot(hid, wdn_ref[c:c + ff_chunk, :])
        y = part if y is None else y + part
    y_ref[...] = _layer_norm(alpha * x1 + y + bdn_ref[...], g2_ref[...], b2_ref[...])


def _out_mlp(attn, rnn, x2d, woa, wor, g1, b1, wup, bup, wdn, bdn, g2, b2, *, alpha, tm,
             attn_channel_major):
    n, dm = x2d.shape
    da = woa.shape[0]
    full = lambda a: pl.BlockSpec(a.shape, lambda i: (0,) * a.ndim)
    attn_spec = (pl.BlockSpec((da, tm), lambda i: (0, i)) if attn_channel_major
                 else pl.BlockSpec((tm, da), lambda i: (i, 0)))
    consts = (woa, wor, g1, b1, wup, bup, wdn, bdn, g2, b2)
    return pl.pallas_call(
        functools.partial(_mlp_body, alpha=alpha, attn_channel_major=attn_channel_major, ff_chunk=1024),
        grid=(n // tm,),
        in_specs=[attn_spec, pl.BlockSpec((tm, rnn.shape[1]), lambda i: (i, 0)),
                  pl.BlockSpec((tm, dm), lambda i: (i, 0))] + [full(a) for a in consts],
        out_specs=pl.BlockSpec((tm, dm), lambda i: (i, 0)),
        out_shape=jax.ShapeDtypeStruct((n, dm), F32),
        compiler_params=pltpu.CompilerParams(dimension_semantics=("arbitrary",),
                                             vmem_limit_bytes=VMEM_LIMIT),
        name="out_mlp",
    )(attn, rnn, x2d, *consts)


def _sample_keys_body(pt_ref, qi_ref, w_ref, kin_ref, cki_ref, keys_ref, knew_ref, kibuf, sem,
                      *, layer):
    b = pl.program_id(0)
    nb = pl.num_programs(0)
    n_pages = kibuf.shape[1]

    def page_copy(bb, slot, p):
        return pltpu.make_async_copy(cki_ref.at[layer, pt_ref[bb, p]], kibuf.at[slot, p], sem.at[slot])

    def issue_all(bb, slot):
        def body(p, _):
            page_copy(bb, slot, p).start()
            return 0
        lax.fori_loop(0, n_pages, body, 0)

    slot = lax.rem(b, 2)

    @pl.when(b == 0)
    def _():
        issue_all(b, slot)

    @pl.when(b + 1 < nb)
    def _():
        issue_all(b + 1, 1 - slot)

    def wait_body(p, _):
        page_copy(b, slot, p).wait()
        return 0

    lax.fori_loop(0, n_pages, wait_body, 0)

    qi16 = qi_ref[0].astype(BF16)
    w = w_ref[0]
    group = 16

    def score_body(g, _):
        for u in range(group):
            p = g * group + u
            d = _dot(qi16, kibuf[slot, p].astype(BF16))
            keys_ref[0, pl.ds(p, 1), :] = jnp.sum(w * jnp.maximum(d, 0.0), axis=0, keepdims=True)
        return 0

    lax.fori_loop(0, n_pages // group, score_body, 0)

    dn = jnp.sum(qi16.astype(F32) * kin_ref[0].astype(BF16).astype(F32), axis=1, keepdims=True)
    score_new = jnp.sum(w * jnp.maximum(dn, 0.0), axis=0, keepdims=True)
    knew_ref[0] = jnp.broadcast_to(score_new, (1, LANES))


def _sample_keys(page_table, qi_s, wi_col, ki_new, cache_ki_t, layer):
    bd, n_pages = page_table.shape
    assert n_pages % 16 == 0
    grid_spec = pltpu.PrefetchScalarGridSpec(
        num_scalar_prefetch=1,
        grid=(bd,),
        in_specs=[
            pl.BlockSpec((1, N_IDX_HEADS, IDX_DIM), lambda b, pt: (b, 0, 0)),
            pl.BlockSpec((1, N_IDX_HEADS, 1), lambda b, pt: (b, 0, 0)),
            pl.BlockSpec((1, 1, IDX_DIM), lambda b, pt: (b, 0, 0)),
            pl.BlockSpec(memory_space=pl.ANY),
        ],
        out_specs=[
            pl.BlockSpec((1, n_pages, PAGE_SIZE), lambda b, pt: (b, 0, 0)),
            pl.BlockSpec((1, 1, LANES), lambda b, pt: (b, 0, 0)),
        ],
        scratch_shapes=[
            pltpu.VMEM((2, n_pages, IDX_DIM, PAGE_SIZE), F32),
            pltpu.SemaphoreType.DMA((2,)),
        ],
    )
    return pl.pallas_call(
        functools.partial(_sample_keys_body, layer=layer),
        grid_spec=grid_spec,
        out_shape=[jax.ShapeDtypeStruct((bd, n_pages, PAGE_SIZE), F32),
                   jax.ShapeDtypeStruct((bd, 1, LANES), F32)],
        compiler_params=pltpu.CompilerParams(dimension_semantics=("arbitrary",),
                                             vmem_limit_bytes=VMEM_LIMIT),
        name="sample_keys",
    )(page_table, qi_s, wi_col, ki_new, cache_ki_t)


def _sample_mask_body(keys_ref, knew_ref, madd_ref, maddn_ref, *, topk, idx_bits):
    keys = keys_ref[...]
    key_new = knew_ref[:, :, 0:1]
    n_past = keys.shape[1] * keys.shape[2]
    kidx = (lax.broadcasted_iota(I32, keys.shape, 1) * PAGE_SIZE
            + lax.broadcasted_iota(I32, keys.shape, 2))

    def count(pred_past, pred_new):
        c = jnp.sum(jnp.where(pred_past, 1, 0).astype(I32), axis=1, keepdims=True)
        return jnp.sum(c, axis=2, keepdims=True) + jnp.where(pred_new, 1, 0).astype(I32)

    c0 = count(keys >= 0.0, key_new >= 0.0)
    nonneg = c0 >= topk
    prefix = jnp.where(nonneg, 0, INT_MIN).astype(I32)
    cge = jnp.where(nonneg, c0, n_past + 1).astype(I32)

    def bit_body(it, carry):
        prefix, cge = carry
        cand = prefix | lax.shift_left(jnp.int32(1), 30 - it)
        cand_f = _key_to_float(cand)
        c = count(keys >= cand_f, key_new >= cand_f)
        ok = c >= topk
        return jnp.where(ok, cand, prefix), jnp.where(ok, c, cge)

    thr_key, cge = lax.fori_loop(0, 31, bit_body, (prefix, cge))
    real = thr_key > NEG_INF_KEY
    thr = jnp.where(real, _key_to_float(thr_key), NEG_INF)
    need = jnp.logical_and(cge > topk, real)
    j_default = jnp.where(real, jnp.int32(2 ** 30), jnp.int32(-1))

    def tie_fn():
        want = topk - count(keys > thr, key_new > thr)

        def body(it, p):
            cand = p | lax.shift_left(jnp.int32(1), idx_bits - 1 - it)
            c = count(jnp.logical_and(keys == thr, kidx < cand),
                      jnp.logical_and(key_new == thr, n_past < cand))
            return jnp.where(c < want, cand, p)

        p = lax.fori_loop(0, idx_bits, body, jnp.zeros(thr_key.shape, I32))
        return jnp.where(need, p, j_default)

    jcut = lax.cond(jnp.max(jnp.where(need, 1, 0)) > 0, tie_fn, lambda: j_default)
    sel = jnp.logical_or(keys > thr, jnp.logical_and(keys == thr, kidx <= jcut))
    sel_new = jnp.logical_or(key_new > thr, jnp.logical_and(key_new == thr, n_past <= jcut))
    madd_ref[...] = jnp.where(sel, 0.0, NEG_INF).astype(F32)
    maddn_ref[...] = jnp.broadcast_to(jnp.where(sel_new, 0.0, NEG_INF).astype(F32), maddn_ref.shape)


def _sample_mask(keys, key_new, topk):
    bd, n_pages, _ = keys.shape
    idx_bits = max(1, (n_pages * PAGE_SIZE).bit_length())
    return pl.pallas_call(
        functools.partial(_sample_mask_body, topk=topk, idx_bits=idx_bits),
        out_shape=[jax.ShapeDtypeStruct(keys.shape, F32), jax.ShapeDtypeStruct(key_new.shape, F32)],
        compiler_params=pltpu.CompilerParams(vmem_limit_bytes=VMEM_LIMIT),
        name="sample_mask",
    )(keys, key_new)


def _sample_attend_body(pt_ref, qb_ref, q_ref, kn_ref, vn_ref, madd_ref, maddn_ref, btab_ref,
                        ck_ref, cv_ref, o_ref, ring_ref, lg_ref, acc_ref, sem, *, layer):
    b = pl.program_id(0)
    n_pages = madd_ref.shape[1]
    depth = ring_ref.shape[0]
    group = 4
    per_seq = 2 * n_pages
    total_items = pl.num_programs(0) * per_seq
    base = b * per_seq

    def item_copy(t, cache_ref, seq_i, page_i):
        slot = lax.rem(t, depth)
        return pltpu.make_async_copy(cache_ref.at[layer, pt_ref[seq_i, page_i]], ring_ref.at[slot],
                                     sem.at[slot])

    def start_item(t):
        seq_i = lax.div(t, per_seq)
        r = t - seq_i * per_seq
        is_v = r >= n_pages

        @pl.when(jnp.logical_not(is_v))
        def _():
            item_copy(t, ck_ref, seq_i, r).start()

        @pl.when(is_v)
        def _():
            item_copy(t, cv_ref, seq_i, r - n_pages).start()

    def finish_item(t):
        item_copy(t, ck_ref, 0, 0).wait()

    def refill(t):
        @pl.when(t + depth < total_items)
        def _():
            start_item(t + depth)

    @pl.when(b == 0)
    def _():
        def prime(t, _):
            start_item(t)
            return 0
        lax.fori_loop(0, depth, prime, 0)

    def k_body(g, _):
        for u in range(group):
            p = g * group + u
            t = base + p
            finish_item(t)
            slot = lax.rem(t, depth)
            mrow = madd_ref[0, pl.ds(p, 1), :]
            is_last = p == n_pages - 1
            for h in range(N_HEADS):
                lg = jnp.sum(ring_ref[slot, h] * qb_ref[0, h], axis=0, keepdims=True) + mrow
                lg_ref[h, pl.ds(p, 1), :] = lg + jnp.where(is_last, btab_ref[h:h + 1, 0:PAGE_SIZE], 0.0)
            refill(t)
        return 0

    lax.fori_loop(0, n_pages // group, k_body, 0)

    def total(x, op):
        return op(op(x, axis=0, keepdims=True), axis=1, keepdims=True)

    lg_new = (jnp.sum(q_ref[0] * kn_ref[0], axis=1, keepdims=True)
              + btab_ref[:, PAGE_SIZE:PAGE_SIZE + 1] + maddn_ref[0][:, 0:1])
    p_new, l = [], []
    for h in range(N_HEADS):
        lgh = lg_ref[h]
        mh = jnp.maximum(total(lgh, jnp.max), lg_new[h:h + 1])
        prh = jnp.exp(lgh - mh)
        lg_ref[h] = prh
        p_new.append(jnp.exp(lg_new[h:h + 1] - mh))
        l.append(total(prh, jnp.sum) + p_new[h])

    acc_ref[...] = jnp.zeros_like(acc_ref)

    def v_body(g, _):
        parts = [None] * N_HEADS
        for u in range(group):
            p = g * group + u
            t = base + n_pages + p
            finish_item(t)
            slot = lax.rem(t, depth)
            for h in range(N_HEADS):
                term = ring_ref[slot, h] * lg_ref[h, pl.ds(p, 1), :]
                parts[h] = term if parts[h] is None else parts[h] + term
            refill(t)
        for h in range(N_HEADS):
            acc_ref[h] += parts[h]
        return 0

    lax.fori_loop(0, n_pages // group, v_body, 0)

    for h in range(N_HEADS):
        past = jnp.sum(acc_ref[h].T, axis=0, keepdims=True)
        o_ref[0, h:h + 1, :] = (past + p_new[h] * vn_ref[0, h:h + 1, :]) / l[h]


def _sample_attend(page_table, q_lanes, q_s, k_new, v_new, madd, madd_new, btab, cache_k_t, cache_v_t,
                   layer):
    bd, n_pages = page_table.shape
    assert n_pages % 4 == 0 and 2 * n_pages >= PAGE_RING
    blk3 = lambda b, pt: (b, 0, 0)
    grid_spec = pltpu.PrefetchScalarGridSpec(
        num_scalar_prefetch=1,
        grid=(bd,),
        in_specs=[
            pl.BlockSpec((1, N_HEADS, HEAD_DIM, PAGE_SIZE), lambda b, pt: (b, 0, 0, 0)),
            pl.BlockSpec((1, N_HEADS, HEAD_DIM), blk3),
            pl.BlockSpec((1, N_HEADS, HEAD_DIM), blk3),
            pl.BlockSpec((1, N_HEADS, HEAD_DIM), blk3),
            pl.BlockSpec((1, n_pages, PAGE_SIZE), blk3),
            pl.BlockSpec((1, 1, LANES), blk3),
            pl.BlockSpec(btab.shape, lambda b, pt: (0, 0)),
            pl.BlockSpec(memory_space=pl.ANY),
            pl.BlockSpec(memory_space=pl.ANY),
        ],
        out_specs=pl.BlockSpec((1, N_HEADS, HEAD_DIM), blk3),
        scratch_shapes=[
            pltpu.VMEM((PAGE_RING, N_HEADS, HEAD_DIM, PAGE_SIZE), F32),
            pltpu.VMEM((N_HEADS, n_pages, PAGE_SIZE), F32),
            pltpu.VMEM((N_HEADS, HEAD_DIM, PAGE_SIZE), F32),
            pltpu.SemaphoreType.DMA((PAGE_RING,)),
        ],
    )
    return pl.pallas_call(
        functools.partial(_sample_attend_body, layer=layer),
        grid_spec=grid_spec,
        out_shape=jax.ShapeDtypeStruct((bd, N_HEADS, HEAD_DIM), F32),
        compiler_params=pltpu.CompilerParams(dimension_semantics=("arbitrary",),
                                             vmem_limit_bytes=VMEM_LIMIT),
        name="sample_attend",
    )(page_table, q_lanes, q_s, k_new, v_new, madd, madd_new, btab, cache_k_t, cache_v_t)


def _block_diag(w):
    nb, c, d = w.shape
    eye = jnp.eye(nb, dtype=w.dtype)
    return (eye[:, None, :, None] * w[:, :, None, :]).reshape(nb * c, nb * d)


def kernel(x_prompt, x_sample, cache_k, cache_v, cache_k_idx, state_h, state_conv, page_table,
           rel_bias, w_in, conv_w, conv_b, w_a, b_a, w_x, b_x, lru_lambda, w_out,
           ln1_g, ln1_b, w_up, b_up, w_down, b_down, ln2_g, ln2_b):
    depth = w_in.shape[0]
    batch, seq, dm = x_prompt.shape
    bd, dec_seq, _ = x_sample.shape
    assert dec_seq == 1, "the decode pass handles one new token per sequence"
    d_rnn = dm - D_ATT
    alpha = (2 * depth) ** 0.25
    n_pages = page_table.shape[1]
    n_past = n_pages * PAGE_SIZE
    topk_s = min(TOPK_MAX, (n_past + dec_seq) // 4)
    row = lambda a: a.reshape(1, -1)
    c16 = lambda a: a.astype(BF16)

    sizes = (D_ATT, D_ATT, D_ATT, N_IDX_HEADS * IDX_DIM, IDX_DIM, N_IDX_HEADS, d_rnn, d_rnn)
    offs = [0]
    for s in sizes:
        offs.append(offs[-1] + s)

    w_in_t = jnp.swapaxes(w_in, 1, 2)
    cache_ki_t = jnp.swapaxes(cache_k_idx, 2, 3)
    cache_k_t = jnp.transpose(cache_k, (0, 1, 3, 4, 2))
    cache_v_t = jnp.transpose(cache_v, (0, 1, 3, 4, 2))

    bias_t = _prompt_bias_tiles(rel_bias)
    shifted = (rel_bias - rel_bias[N_BUCKETS - 1:N_BUCKETS]).T.astype(F32)
    btab = jnp.concatenate([
        _bucket_lookup(shifted, _t5_bucket(PAGE_SIZE - jnp.arange(PAGE_SIZE, dtype=I32))),
        _bucket_lookup(shifted, _t5_bucket(jnp.zeros((PAGE_SIZE,), I32)))], axis=1)

    yp = x_prompt.reshape(batch * seq, dm)
    ys = x_sample.reshape(bd, dm)
    outs = [[] for _ in range(10)]
    for layer in range(depth):
        wt = w_in_t[layer]
        wq, wk, wv, wqi, wki, wwi, wxr, wgr = [wt[offs[j]:offs[j + 1]] for j in range(8)]
        cw, cb, lam = conv_w[layer], row(conv_b[layer]), row(lru_lambda[layer])
        wa_bd, wx_bd = _block_diag(w_a[layer]), _block_diag(w_x[layer])
        ba, bx = row(b_a[layer]), row(b_x[layer])
        woa, wor = c16(w_out[layer][:D_ATT]), c16(w_out[layer][D_ATT:])
        mlp_consts = (woa, wor, row(ln1_g[layer]), row(ln1_b[layer]), c16(w_up[layer]), row(b_up[layer]),
                      c16(w_down[layer]), row(b_down[layer]), row(ln2_g[layer]), row(ln2_b[layer]))

        p_weights = [c16(wq), c16(wk), c16(wk), c16(wv), c16(wqi), c16(jnp.concatenate([wki, wki], 0)),
                     c16(wki), c16(wwi), c16(wxr), c16(wgr)]
        p_plan = (
            (False, ((0, D_ATT, HEAD_DIM ** -0.5, BF16),)),
            (False, ((0, D_ATT, 1.0, BF16),)),
            (True, ((0, D_ATT, 1.0, F32),)),
            (True, ((0, D_ATT, 1.0, F32), (0, D_ATT, 1.0, BF16))),
            (False, ((0, N_IDX_HEADS * IDX_DIM, 1.0, BF16),)),
            (False, ((0, 2 * IDX_DIM, 1.0, BF16),)),
            (True, ((0, IDX_DIM, 1.0, F32),)),
            (True, ((0, N_IDX_HEADS, 1.0, F32),)),
            (False, ((0, d_rnn, 1.0, F32),)),
            (False, ((0, d_rnn, 1.0, F32),)),
        )
        qb, kb, k_t, v_t, vtb, qib, ki2b, ki_t, wit, xr_p, gr_p = _project(
            yp, p_weights, p_plan, batch, seq, tm=512)
        attn_t = _prompt_attention(qb, qib, wit, kb, ki2b, vtb, bias_t, batch, seq)
        rnn_p, h_p = _rglru_prompt(xr_p, gr_p, cw, cb, c16(wa_bd), ba, c16(wx_bd), bx, lam,
                                   batch, seq, tt=512)
        conv_p = xr_p.reshape(batch, seq, d_rnn)[:, seq - (CONV_W - 1):]
        yp = _out_mlp(attn_t, rnn_p, yp, *mlp_consts, alpha=alpha, tm=512, attn_channel_major=True)

        pad_wi = jnp.pad(wwi, ((0, LANES - N_IDX_HEADS), (0, 0)))
        s_weights = [wq, wk, wv, wqi, wki, pad_wi, wxr, wgr]
        s_plan = (
            (False, ((0, D_ATT, HEAD_DIM ** -0.5, F32),)),
            (False, ((0, D_ATT, 1.0, F32),)),
            (False, ((0, D_ATT, 1.0, F32),)),
            (False, ((0, N_IDX_HEADS * IDX_DIM, 1.0, F32),)),
            (False, ((0, IDX_DIM, 1.0, F32),)),
            (False, ((0, LANES, 1.0, F32),)),
            (False, ((0, d_rnn, 1.0, F32),)),
            (False, ((0, d_rnn, 1.0, F32),)),
        )
        q_s, k_s, v_s, qi_s, ki_s, wi_s, xr_s, gr_s = _project(ys, s_weights, s_plan, 1, bd, tm=bd,
                                                               precision=lax.Precision.HIGHEST)
        keys, key_new = _sample_keys(page_table, qi_s.reshape(bd, N_IDX_HEADS, IDX_DIM),
                                     wi_s[:, :N_IDX_HEADS].reshape(bd, N_IDX_HEADS, 1),
                                     ki_s.reshape(bd, 1, IDX_DIM), cache_ki_t, layer)
        madd, madd_new = _sample_mask(keys, key_new, topk_s)
        heads = lambda a: a.reshape(bd, N_HEADS, HEAD_DIM)
        q_lanes = jnp.broadcast_to(heads(q_s)[..., None], (bd, N_HEADS, HEAD_DIM, PAGE_SIZE))
        attn_s = _sample_attend(page_table, q_lanes, heads(q_s), heads(k_s), heads(v_s), madd, madd_new,
                                btab, cache_k_t, cache_v_t, layer)
        conv_prev = state_conv[layer]
        rnn_s, h_s = _rglru_sample(xr_s, gr_s, jnp.swapaxes(conv_prev, 0, 1), state_h[layer], cw, cb,
                                   wa_bd, ba, wx_bd, bx, lam)
        conv_s = jnp.concatenate([conv_prev[:, 1:], xr_s[:, None, :]], axis=1)
        ys = _out_mlp(attn_s.reshape(bd, D_ATT), rnn_s, ys, *mlp_consts, alpha=alpha, tm=bd,
                      attn_channel_major=False)

        per_head_t = lambda a: jnp.transpose(a.reshape(batch, N_HEADS, HEAD_DIM, seq), (0, 3, 1, 2))
        layer_outs = (
            per_head_t(k_t), per_head_t(v_t), jnp.swapaxes(ki_t, 1, 2), h_p.reshape(batch, d_rnn), conv_p,
            k_s.reshape(bd, dec_seq, N_HEADS, HEAD_DIM), v_s.reshape(bd, dec_seq, N_HEADS, HEAD_DIM),
            ki_s.reshape(bd, dec_seq, IDX_DIM), h_s, conv_s,
        )
        for lst, a in zip(outs, layer_outs):
            lst.append(a)

    stacked = [jnp.stack(lst) for lst in outs]
    return (yp.reshape(batch, seq, dm), ys.reshape(bd, dec_seq, dm), *stacked)
```
